```python
import math
import jax, jax.numpy as jnp
from jax import lax
import numpy as np

D_MODEL = 1024
BATCH = 16
SEQ = 2048
DEPTH = 1
DEC_BATCH = 128
DEC_SEQ = 4
PAST_LEN = 8192
PAGE_SIZE = 128

N_META = 16
MLA_HEADS = 8
QK_NOPE = 64
QK_ROPE = 32
V_DIM = 64
Q_RANK = 256
KV_RANK = 128
ROPE_THETA = 10000.0
Q_BLOCK = 128
SM_SCALE = (QK_NOPE + QK_ROPE) ** -0.5
ML_HEADS = 4
ML_DH = 128
ML_W = ML_HEADS * ML_DH
CHUNK = 64
MLA_W = MLA_HEADS * V_DIM
MIX_W = MLA_W + ML_W
O_CQ = 0
O_CKV = O_CQ + Q_RANK
O_KR = O_CKV + KV_RANK
O_MQ = O_KR + QK_ROPE
O_MK = O_MQ + ML_W
O_MV = O_MK + ML_W
O_MO = O_MV + ML_W
O_G = O_MO + ML_W
IN_W = O_G + 2 * ML_HEADS
PEER_HEADS = 8
N_KEYS = 128
N_EXPERTS = N_KEYS * N_KEYS
PEER_DQ = 256
PEER_TOPK = 16
PEER_BLOCK = 256
DN_ALPHA = (2.0 * DEPTH) ** 0.25
DN_BETA = (8.0 * DEPTH) ** -0.25
EPS = 1e-5

kernel_name = "hymba_mla_mlstm_peer_step"


def layer_norm(x, g, b):
    xf = x.astype(jnp.float32)
    mu = xf.mean(-1, keepdims=True)
    var = jnp.square(xf - mu).mean(-1, keepdims=True)
    return ((xf - mu) * lax.rsqrt(var + EPS) * g + b).astype(x.dtype)


def rms_norm(x, g):
    xf = x.astype(jnp.float32)
    return (xf * lax.rsqrt(jnp.square(xf).mean(-1, keepdims=True) + 1e-6) * g).astype(x.dtype)


def rope(x, pos):
    half = QK_ROPE // 2
    inv = ROPE_THETA ** (-jnp.arange(half, dtype=jnp.float32) / half)
    ang = pos.astype(jnp.float32)[:, None] * inv
    ang = ang.reshape((ang.shape[0],) + (1,) * (x.ndim - 3) + (half,))
    cos, sin = jnp.cos(ang), jnp.sin(ang)
    xf = x.astype(jnp.float32)
    x1, x2 = xf[..., :half], xf[..., half:]
    return jnp.concatenate([x1 * cos - x2 * sin, x1 * sin + x2 * cos], axis=-1).astype(x.dtype)


def mixer_project(h, pos, w_in, b_gates, q_norm_g, w_qb, kv_norm_g, w_kvb):
    B, T, _ = h.shape
    z = jnp.einsum('btd,de->bte', h, w_in)
    c_q = z[..., O_CQ:O_CKV]
    c_kv = z[..., O_CKV:O_KR]
    k_r = z[..., O_KR:O_MQ]
    q = jnp.einsum('btr,re->bte', rms_norm(c_q, q_norm_g), w_qb).reshape(B, T, MLA_HEADS, QK_NOPE + QK_ROPE)
    q_nope, q_rope = q[..., :QK_NOPE], rope(q[..., QK_NOPE:], pos)
    w_uk = w_kvb.reshape(KV_RANK, MLA_HEADS, QK_NOPE + V_DIM)[..., :QK_NOPE]
    q_lat = jnp.einsum('bthd,chd->bthc', q_nope, w_uk)
    ckv = rms_norm(c_kv, kv_norm_g)
    kpe = rope(k_r, pos)
    mq = z[..., O_MQ:O_MK].reshape(B, T, ML_HEADS, ML_DH)
    mk = z[..., O_MK:O_MV].reshape(B, T, ML_HEADS, ML_DH) * (ML_DH ** -0.5)
    mv = z[..., O_MV:O_MO].reshape(B, T, ML_HEADS, ML_DH)
    og = jax.nn.sigmoid(z[..., O_MO:O_G])
    gates = z[..., O_G:].astype(jnp.float32) + b_gates.astype(jnp.float32)
    ig = gates[..., :ML_HEADS]
    lf = jax.nn.log_sigmoid(gates[..., ML_HEADS:])
    return q_lat, q_rope, ckv, kpe, mq, mk, mv, ig, lf, og


def mla_attend(q_lat, q_rope, q_pos, ckv, kpe, k_pos):
    s = jnp.einsum('bqhc,bkc->bhqk', q_lat, ckv) + jnp.einsum('bqhr,bkr->bhqk', q_rope, kpe)
    s = s.astype(jnp.float32) * SM_SCALE
    s = jnp.where(k_pos[None, :] <= q_pos[:, None], s, -jnp.inf)
    p = jax.nn.softmax(s, axis=-1).astype(ckv.dtype)
    return jnp.einsum('bhqk,bkc->bqhc', p, ckv)


def mla_prompt(q_lat, q_rope, ckv, kpe):
    B, T, H, C = q_lat.shape
    n_blk = -(-T // Q_BLOCK)
    pad = n_blk * Q_BLOCK - T
    qlb = jnp.pad(q_lat, ((0, 0), (0, pad), (0, 0), (0, 0))).reshape(B, n_blk, Q_BLOCK, H, C).transpose(1, 0, 2, 3, 4)
    qrb = jnp.pad(q_rope, ((0, 0), (0, pad), (0, 0), (0, 0))).reshape(B, n_blk, Q_BLOCK, H, QK_ROPE).transpose(1, 0, 2, 3, 4)
    pos_b = jnp.arange(n_blk * Q_BLOCK).reshape(n_blk, Q_BLOCK)
    k_pos = jnp.arange(T)
    out = lax.map(lambda a: mla_attend(a[0], a[1], a[2], ckv, kpe, k_pos), (qlb, qrb, pos_b))
    return out.transpose(1, 0, 2, 3, 4).reshape(B, n_blk * Q_BLOCK, H, C)[:, :T]


def mlstm_chunk(state, q, k, v, ig, lf):
    C0, n0, m0 = state
    q, k, v = (a.astype(jnp.float32) for a in (q, k, v))
    L = q.shape[1]
    b = jnp.cumsum(lf, axis=1).transpose(0, 2, 1)
    igh = ig.transpose(0, 2, 1)
    causal = jnp.tril(jnp.ones((L, L), dtype=bool))
    D = jnp.where(causal, b[..., :, None] - b[..., None, :] + igh[..., None, :], -jnp.inf)
    inter = b + m0[..., None]
    m = jnp.maximum(inter, D.max(-1))
    w = jnp.exp(D - m[..., None])
    a = jnp.exp(inter - m)
    wqk = w * jnp.einsum('blhd,bshd->bhls', q, k)
    num = a[..., None] * jnp.einsum('bhvk,blhk->bhlv', C0, q) + jnp.einsum('bhls,bshv->bhlv', wqk, v)
    den = a * jnp.einsum('bhk,blhk->bhl', n0, q) + wqk.sum(-1)
    h = num / jnp.maximum(jnp.abs(den), jnp.exp(-m))[..., None]
    bL = b[..., -1]
    dec = bL[..., None] - b + igh
    mL = jnp.maximum(bL + m0, dec.max(-1))
    ws = jnp.exp(dec - mL[..., None])
    aL = jnp.exp(bL + m0 - mL)
    CL = aL[..., None, None] * C0 + jnp.einsum('bhs,bshv,bshk->bhvk', ws, v, k)
    nL = aL[..., None] * n0 + jnp.einsum('bhs,bshk->bhk', ws, k)
    return (CL, nL, mL), h.transpose(0, 2, 1, 3)


def mlstm_prompt(q, k, v, ig, lf):
    B, T = q.shape[:2]
    f32 = jnp.float32
    state = (jnp.zeros((B, ML_HEADS, ML_DH, ML_DH), f32), jnp.zeros((B, ML_HEADS, ML_DH), f32),
             jnp.zeros((B, ML_HEADS), f32))
    state, h_meta = mlstm_chunk(state, q[:, :N_META], k[:, :N_META], v[:, :N_META], ig[:, :N_META], lf[:, :N_META])
    nc = (T - N_META) // CHUNK

    def split(a):
        a = a[:, N_META:]
        return a.reshape((B, nc, CHUNK) + a.shape[2:]).swapaxes(0, 1)

    state, h_rest = lax.scan(lambda s, xs: mlstm_chunk(s, *xs), state, tuple(split(a) for a in (q, k, v, ig, lf)))
    h_rest = h_rest.swapaxes(0, 1).reshape(B, nc * CHUNK, ML_HEADS, ML_DH)
    return state, jnp.concatenate([h_meta, h_rest], axis=1)


def mixer_output(o_lat, h_ml, og, w_kvb, mh_norm_g, w_o, dtype):
    B, T = o_lat.shape[:2]
    w_uv = w_kvb.reshape(KV_RANK, MLA_HEADS, QK_NOPE + V_DIM)[..., QK_NOPE:]
    o_mla = jnp.einsum('bthc,chv->bthv', o_lat, w_uv).reshape(B, T, MLA_W)
    mu = h_ml.mean(-1, keepdims=True)
    var = jnp.square(h_ml - mu).mean(-1, keepdims=True)
    hn = ((h_ml - mu) * lax.rsqrt(var + EPS)).reshape(B, T, ML_W)
    o_ml = (hn * mh_norm_g).astype(dtype) * og
    return jnp.einsum('bte,ed->btd', jnp.concatenate([o_mla.astype(dtype), o_ml], axis=-1), w_o)


def peer_block(x, w_pq, sub_keys, expert_u, expert_v):
    n = x.shape[0]
    q = jnp.einsum('nd,de->ne', x, w_pq).reshape(n, PEER_HEADS, 2, PEER_DQ // 2)
    s = jnp.einsum('nhpc,hpkc->nhpk', q, sub_keys).astype(jnp.float32)
    sv, si = lax.top_k(s, PEER_TOPK)
    cand = (sv[:, :, 0, :, None] + sv[:, :, 1, None, :]).reshape(n, PEER_HEADS, PEER_TOPK * PEER_TOPK)
    cv, ci = lax.top_k(cand, PEER_TOPK)
    e1 = jnp.take_along_axis(si[:, :, 0], ci // PEER_TOPK, axis=-1)
    e2 = jnp.take_along_axis(si[:, :, 1], ci % PEER_TOPK, axis=-1)
    experts = e1 * N_KEYS + e2
    g = jax.nn.softmax(cv, axis=-1)
    act = jax.nn.gelu(jnp.einsum('nhkd,nd->nhk', expert_u[experts], x).astype(jnp.float32))
    return jnp.einsum('nhk,nhkd->nd', (g * act).astype(x.dtype), expert_v[experts])


def peer(h, w_pq, sub_keys, expert_u, expert_v):
    shp = h.shape
    x = h.reshape(-1, D_MODEL)
    N = x.shape[0]
    nb = -(-N // PEER_BLOCK)
    xb = jnp.pad(x, ((0, nb * PEER_BLOCK - N), (0, 0))).reshape(nb, PEER_BLOCK, D_MODEL)
    out = lax.map(lambda xx: peer_block(xx, w_pq, sub_keys, expert_u, expert_v), xb)
    return out.reshape(-1, D_MODEL)[:N].reshape(shp)


def setup_inputs(seed: int = 0) -> dict:
    key = jax.random.key(seed)
    ks = jax.random.split(key, 32)
    nrm = jax.random.normal
    f32 = jnp.float32
    n_pages = PAST_LEN // PAGE_SIZE
    n_used = DEC_BATCH * n_pages
    n_pool = n_used + n_used // 4
    page_table = jax.random.permutation(ks[0], n_pool)[:n_used].reshape(DEC_BATCH, n_pages).astype(jnp.int32)

    def gain(k, shape):
        return 1.0 + 0.02 * nrm(k, shape, f32)

    b_gates = jnp.concatenate([0.1 * nrm(ks[1], (DEPTH, ML_HEADS), f32),
                               jnp.linspace(3.0, 6.0, ML_HEADS, dtype=f32)[None] + 0.1 * nrm(ks[2], (DEPTH, ML_HEADS), f32)], axis=-1)
    return {
        "x_prompt": nrm(ks[3], (BATCH, SEQ, D_MODEL), f32),
        "x_sample": nrm(ks[4], (DEC_BATCH, DEC_SEQ, D_MODEL), f32),
        "cache_ckv": nrm(ks[5], (DEPTH, n_pool, PAGE_SIZE, KV_RANK), f32),
        "cache_kpe": nrm(ks[6], (DEPTH, n_pool, PAGE_SIZE, QK_ROPE), f32),
        "page_table": page_table,
        "state_C": 0.1 * nrm(ks[7], (DEPTH, DEC_BATCH, ML_HEADS, ML_DH, ML_DH), f32),
        "state_n": 0.1 * nrm(ks[8], (DEPTH, DEC_BATCH, ML_HEADS, ML_DH), f32),
        "state_m": nrm(ks[9], (DEPTH, DEC_BATCH, ML_HEADS), f32),
        "meta_tokens": nrm(ks[10], (N_META, D_MODEL), f32),
        "ln_in_g": gain(ks[11], (D_MODEL,)),
        "ln_in_b": 0.02 * nrm(ks[12], (D_MODEL,), f32),
        "w_in": nrm(ks[13], (DEPTH, D_MODEL, IN_W), f32) * D_MODEL ** -0.5,
        "b_gates": b_gates,
        "q_norm_g": gain(ks[14], (DEPTH, Q_RANK)),
        "w_qb": nrm(ks[15], (DEPTH, Q_RANK, MLA_HEADS * (QK_NOPE + QK_ROPE)), f32) * Q_RANK ** -0.5,
        "kv_norm_g": gain(ks[16], (DEPTH, KV_RANK)),
        "w_kvb": nrm(ks[17], (DEPTH, KV_RANK, MLA_HEADS * (QK_NOPE + V_DIM)), f32) * KV_RANK ** -0.5,
        "mh_norm_g": gain(ks[18], (DEPTH, ML_W)),
        "w_o": nrm(ks[19], (DEPTH, MIX_W, D_MODEL), f32) * (MIX_W ** -0.5 * DN_BETA),
        "ln1_g": gain(ks[20], (DEPTH, D_MODEL)),
        "ln1_b": 0.02 * nrm(ks[21], (DEPTH, D_MODEL), f32),
        "w_pq": nrm(ks[22], (DEPTH, D_MODEL, PEER_HEADS * PEER_DQ), f32) * D_MODEL ** -0.5,
        "sub_keys": nrm(ks[23], (DEPTH, PEER_HEADS, 2, N_KEYS, PEER_DQ // 2), f32) * (PEER_DQ // 2) ** -0.5,
        "expert_u": nrm(ks[24], (DEPTH, N_EXPERTS, D_MODEL), f32) * D_MODEL ** -0.5,
        "expert_v": nrm(ks[25], (DEPTH, N_EXPERTS, D_MODEL), f32) * DN_BETA,
        "ln2_g": gain(ks[26], (DEPTH, D_MODEL)),
        "ln2_b": 0.02 * nrm(ks[27], (DEPTH, D_MODEL), f32),
    }


def reference(x_prompt, x_sample, cache_ckv, cache_kpe, page_table, state_C, state_n, state_m,
              meta_tokens, ln_in_g, ln_in_b, w_in, b_gates, q_norm_g, w_qb, kv_norm_g, w_kvb,
              mh_norm_g, w_o, ln1_g, ln1_b, w_pq, sub_keys, expert_u, expert_v, ln2_g, ln2_b):
    B = x_prompt.shape[0]
    DB, S = x_sample.shape[:2]
    dt = x_prompt.dtype
    meta = jnp.broadcast_to(meta_tokens[None].astype(dt), (B, N_META, D_MODEL))
    hp = layer_norm(jnp.concatenate([meta, x_prompt], axis=1), ln_in_g, ln_in_b)
    hs = layer_norm(x_sample, ln_in_g, ln_in_b)
    T = hp.shape[1]
    past_len = page_table.shape[1] * cache_ckv.shape[2]
    pos_p = jnp.arange(T)
    pos_s = past_len + jnp.arange(S)
    k_pos_s = jnp.arange(past_len + S)
    ckv_p_l, kpe_p_l, C_p_l, n_p_l, m_p_l = [], [], [], [], []
    ckv_s_l, kpe_s_l, C_s_l, n_s_l, m_s_l = [], [], [], [], []
    for l in range(DEPTH):
        q_lat, q_rope, ckv, kpe, mq, mk, mv, ig, lf, og = mixer_project(
            hp, pos_p, w_in[l], b_gates[l], q_norm_g[l], w_qb[l], kv_norm_g[l], w_kvb[l])
        o_lat = mla_prompt(q_lat, q_rope, ckv, kpe)
        (Cp, np_, mp), h_ml = mlstm_prompt(mq, mk, mv, ig, lf)
        a_p = mixer_output(o_lat, h_ml, og, w_kvb[l], mh_norm_g[l], w_o[l], dt)
        hp = layer_norm(DN_ALPHA * hp + a_p, ln1_g[l], ln1_b[l])
        hp = layer_norm(DN_ALPHA * hp + peer(hp, w_pq[l], sub_keys[l], expert_u[l], expert_v[l]), ln2_g[l], ln2_b[l])
        ckv_p_l.append(ckv); kpe_p_l.append(kpe)
        C_p_l.append(Cp.astype(dt)); n_p_l.append(np_.astype(dt)); m_p_l.append(mp.astype(dt))
        q_lat, q_rope, ckv, kpe, mq, mk, mv, ig, lf, og = mixer_project(
            hs, pos_s, w_in[l], b_gates[l], q_norm_g[l], w_qb[l], kv_norm_g[l], w_kvb[l])
        ckv_all = jnp.concatenate([cache_ckv[l][page_table].reshape(DB, past_len, KV_RANK).astype(ckv.dtype), ckv], axis=1)
        kpe_all = jnp.concatenate([cache_kpe[l][page_table].reshape(DB, past_len, QK_ROPE).astype(kpe.dtype), kpe], axis=1)
        o_lat = mla_attend(q_lat, q_rope, pos_s, ckv_all, kpe_all, k_pos_s)
        st = (state_C[l].astype(jnp.float32), state_n[l].astype(jnp.float32), state_m[l].astype(jnp.float32))
        (Cs, ns, ms), h_ml = mlstm_chunk(st, mq, mk, mv, ig, lf)
        a_s = mixer_output(o_lat, h_ml, og, w_kvb[l], mh_norm_g[l], w_o[l], dt)
        hs = layer_norm(DN_ALPHA * hs + a_s, ln1_g[l], ln1_b[l])
        hs = layer_norm(DN_ALPHA * hs + peer(hs, w_pq[l], sub_keys[l], expert_u[l], expert_v[l]), ln2_g[l], ln2_b[l])
        ckv_s_l.append(ckv); kpe_s_l.append(kpe)
        C_s_l.append(Cs.astype(dt)); n_s_l.append(ns.astype(dt)); m_s_l.append(ms.astype(dt))
    y_prompt = hp[:, N_META:]
    y_sample = hs
    return (y_prompt, y_sample,
            jnp.stack(ckv_p_l), jnp.stack(kpe_p_l), jnp.stack(C_p_l), jnp.stack(n_p_l), jnp.stack(m_p_l),
            jnp.stack(ckv_s_l), jnp.stack(kpe_s_l), jnp.stack(C_s_l), jnp.stack(n_s_l), jnp.stack(m_s_l))
```

```python
import functools

import jax
import jax.numpy as jnp
from jax import lax
from jax.experimental import pallas as pl
from jax.experimental.pallas import tpu as pltpu

F32 = jnp.float32
BF16 = jnp.bfloat16

N_META = 16
MLA_HEADS = 8
QK_NOPE = 64
QK_ROPE = 32
V_DIM = 64
Q_RANK = 256
KV_RANK = 128
ROPE_THETA = 10000.0
SM_SCALE = (QK_NOPE + QK_ROPE) ** -0.5
ML_HEADS = 4
ML_DH = 128
ML_W = ML_HEADS * ML_DH
MLA_W = MLA_HEADS * V_DIM
PEER_HEADS = 8
N_KEYS = 128
PEER_DQ = 256
PEER_TOPK = 16
EPS = 1e-5

LANES = 128
SUBLANES = 8
ROW_BLOCK = 256
ATT_TQ = 128
ATT_TK = 256
ML_CHUNK = 128
PAGES_PER_STEP = 8
PEER_TB = 128
NEG_BIG = -1e30
VMEM_LIMIT = 48 * 1024 * 1024

ZW = 2560
Z_CQ, Z_CKV, Z_SMALL, Z_MQ, Z_MK, Z_MV, Z_MO = 0, 256, 384, 512, 1024, 1536, 2048


def _dot(a, b):
    return jnp.dot(a, b, preferred_element_type=F32)


def _dot_nt(a, b):
    return lax.dot_general(a, b, (((1,), (1,)), ((), ())), preferred_element_type=F32)


def _dot_tn(a, b):
    return lax.dot_general(a, b, (((0,), (0,)), ((), ())), preferred_element_type=F32)


def _split3(x):
    a = x.astype(BF16)
    r = x - a.astype(F32)
    b = r.astype(BF16)
    c = (r - b.astype(F32)).astype(BF16)
    return a, b, c


def _dot_exact_rhs01(x, sel):
    a, b, c = _split3(x)
    return _dot(a, sel) + _dot(b, sel) + _dot(c, sel)


def _imod(i, n):
    if n & (n - 1) == 0:
        return jnp.bitwise_and(i, n - 1)
    f = i.astype(F32)
    return (f - n * jnp.floor(f / n)).astype(jnp.int32)


def _layer_norm(x, g, b):
    mu = jnp.mean(x, axis=-1, keepdims=True)
    xc = x - mu
    var = jnp.mean(xc * xc, axis=-1, keepdims=True)
    return xc * lax.rsqrt(var + EPS) * g + b


def _rms_norm(x, g):
    return x * lax.rsqrt(jnp.mean(x * x, axis=-1, keepdims=True) + 1e-6) * g


def _cparams(sem, vmem=VMEM_LIMIT):
    return pltpu.CompilerParams(dimension_semantics=sem, vmem_limit_bytes=vmem)


def _full(shape):
    n = len(shape)
    return pl.BlockSpec(shape, lambda *_: (0,) * n)


def _proj_kernel(x_ref, cos_ref, sin_ref, lng_ref, lnb_ref, win_ref, qng_ref, wqb_ref, wuk_ref, kvg_ref,
                 h_ref, ckv_ref, kpe_ref, gates_ref, qlat_ref, qrope_ref, mq_ref, mk_ref, mv_ref, og_ref,
                 *, apply_ln):
    x = x_ref[...]
    h = _layer_norm(x, lng_ref[...], lnb_ref[...]) if apply_ln else x
    h_ref[...] = h
    z = _dot(h.astype(BF16), win_ref[...])
    ckv_ref[...] = _rms_norm(z[:, Z_CKV:Z_SMALL], kvg_ref[...])
    q = _dot(_rms_norm(z[:, Z_CQ:Z_CKV], qng_ref[...]).astype(BF16), wqb_ref[...])
    nope_w = MLA_HEADS * QK_NOPE
    qlat_ref[...] = _dot(q[:, :nope_w].astype(BF16), wuk_ref[...]).astype(BF16)
    c = cos_ref[...]
    s = sin_ref[...]
    x1 = q[:, nope_w:nope_w + LANES]
    x2 = q[:, nope_w + LANES:nope_w + 2 * LANES]
    qrope_ref[:, 0:LANES] = (x1 * c - x2 * s).astype(BF16)
    qrope_ref[:, LANES:2 * LANES] = (x1 * s + x2 * c).astype(BF16)
    zs = z[:, Z_SMALL:Z_MQ]
    half = QK_ROPE // 2
    lane = lax.broadcasted_iota(jnp.int32, zs.shape, 1)
    first = lane < half
    partner = jnp.where(first, pltpu.roll(zs, LANES - half, 1), pltpu.roll(zs, half, 1))
    kpe = zs * c + partner * jnp.where(first, -s, s)
    kpe_ref[...] = kpe[:, 0:QK_ROPE]
    gates_ref[...] = zs
    mq_ref[...] = z[:, Z_MQ:Z_MK].astype(BF16)
    mk_ref[...] = (z[:, Z_MK:Z_MV] * (ML_DH ** -0.5)).astype(BF16)
    mv_ref[...] = z[:, Z_MV:Z_MO].astype(BF16)
    og_ref[...] = jax.nn.sigmoid(z[:, Z_MO:ZW])


def _proj(x, cos_t, sin_t, lng, lnb, win, qng, wqb, wuk, kvg, *, apply_ln):
    n, d = x.shape
    tm = ROW_BLOCK
    rows = lambda w: pl.BlockSpec((tm, w), lambda i: (i, 0))
    out_w = [(d, F32), (KV_RANK, F32), (QK_ROPE, F32), (LANES, F32), (MLA_HEADS * KV_RANK, BF16),
             (2 * LANES, BF16), (ML_W, BF16), (ML_W, BF16), (ML_W, BF16), (ML_W, F32)]
    return pl.pallas_call(
        functools.partial(_proj_kernel, apply_ln=apply_ln),
        grid=(n // tm,),
        in_specs=[rows(d), rows(LANES), rows(LANES), _full(lng.shape), _full(lnb.shape), _full(win.shape),
                  _full(qng.shape), _full(wqb.shape), _full(wuk.shape), _full(kvg.shape)],
        out_specs=[rows(w) for w, _ in out_w],
        out_shape=[jax.ShapeDtypeStruct((n, w), dt) for w, dt in out_w],
        compiler_params=_cparams(("parallel",)),
        name="proj",
    )(x, cos_t, sin_t, lng, lnb, win, qng, wqb, wuk, kvg)


def _attn_prompt_kernel(q_ref, k_ref, o_ref, m_sc, l_sc, acc_sc):
    i = pl.program_id(1)
    nh, tq, dk = q_ref.shape[1:]
    q = q_ref[0].reshape(nh * tq, dk)
    m_sc[...] = jnp.full(m_sc.shape, -jnp.inf, F32)
    l_sc[...] = jnp.zeros(l_sc.shape, F32)
    acc_sc[...] = jnp.zeros(acc_sc.shape, F32)
    q_pos = i * tq + _imod(lax.broadcasted_iota(jnp.int32, (nh * tq, ATT_TK), 0), tq)
    col = lax.broadcasted_iota(jnp.int32, (nh * tq, ATT_TK), 1)
    n_chunks = ((i + 1) * tq + ATT_TK - 1) // ATT_TK

    def body(c, carry):
        k = k_ref[0, pl.ds(pl.multiple_of(c * ATT_TK, ATT_TK), ATT_TK), :]
        s = _dot_nt(q, k) * SM_SCALE
        s = jnp.where(col + c * ATT_TK <= q_pos, s, -jnp.inf)
        m_prev = m_sc[...]
        m_new = jnp.maximum(m_prev, jnp.max(s, axis=1, keepdims=True))
        alpha = jnp.exp(m_prev - m_new)
        p = jnp.exp(s - m_new)
        l_sc[...] = alpha * l_sc[...] + jnp.sum(p, axis=1, keepdims=True)
        acc_sc[...] = alpha * acc_sc[...] + _dot(p.astype(BF16), k[:, :KV_RANK])
        m_sc[...] = m_new
        return carry

    lax.fori_loop(0, n_chunks, body, 0)
    o_ref[0] = (acc_sc[...] / l_sc[...]).reshape(nh, tq, KV_RANK).astype(o_ref.dtype)


def _attn_prompt(q, k):
    b, nh, tp, dk = q.shape
    tq = ATT_TQ
    return pl.pallas_call(
        _attn_prompt_kernel,
        grid=(b, tp // tq),
        in_specs=[pl.BlockSpec((1, nh, tq, dk), lambda bi, i: (bi, 0, i, 0)),
                  pl.BlockSpec((1, tp, dk), lambda bi, i: (bi, 0, 0))],
        out_specs=pl.BlockSpec((1, nh, tq, KV_RANK), lambda bi, i: (bi, 0, i, 0)),
        out_shape=jax.ShapeDtypeStruct((b, nh, tp, KV_RANK), BF16),
        scratch_shapes=[pltpu.VMEM((nh * tq, 1), F32), pltpu.VMEM((nh * tq, 1), F32),
                        pltpu.VMEM((nh * tq, KV_RANK), F32)],
        compiler_params=_cparams(("parallel", "arbitrary")),
        name="attn_prompt",
    )(q, k)


def _attn_sample_kernel(pt_ref, q_ref, knew_ref, *refs, n_new):
    pps = PAGES_PER_STEP
    ckv_refs, kpe_refs = refs[:pps], refs[pps:2 * pps]
    o_ref, m_sc, l_sc, acc_sc = refs[2 * pps:]
    g = pl.program_id(1)
    q = q_ref[0]
    ql, qr = q[:, :KV_RANK], q[:, KV_RANK:]

    @pl.when(g == 0)
    def _():
        m_sc[...] = jnp.full(m_sc.shape, -jnp.inf, F32)
        l_sc[...] = jnp.zeros(l_sc.shape, F32)
        acc_sc[...] = jnp.zeros(acc_sc.shape, F32)

    def update(s, vals):
        m_prev = m_sc[...]
        m_new = jnp.maximum(m_prev, jnp.max(s, axis=1, keepdims=True))
        alpha = jnp.exp(m_prev - m_new)
        p = jnp.exp(s - m_new)
        l_sc[...] = alpha * l_sc[...] + jnp.sum(p, axis=1, keepdims=True)
        acc = alpha * acc_sc[...]
        w = p.shape[1] // len(vals)
        for j, v in enumerate(vals):
            acc = acc + _dot(p[:, j * w:(j + 1) * w].astype(BF16), v)
        acc_sc[...] = acc
        m_sc[...] = m_new

    kc = [r[...].astype(BF16) for r in ckv_refs]
    s = jnp.concatenate([_dot_nt(ql, kc[j]) + _dot_nt(qr, kpe_refs[j][...].astype(BF16)) for j in range(pps)],
                        axis=1) * SM_SCALE
    update(s, kc)

    @pl.when(g == pl.num_programs(1) - 1)
    def _():
        kn = knew_ref[0]
        s2 = _dot_nt(q, kn) * SM_SCALE
        row = lax.broadcasted_iota(jnp.int32, s2.shape, 0)
        colk = lax.broadcasted_iota(jnp.int32, s2.shape, 1)
        s2 = jnp.where((colk <= _imod(row, n_new)) & (colk < n_new), s2, -jnp.inf)
        update(s2, [kn[:, :KV_RANK]])
        o_ref[0] = (acc_sc[...] / l_sc[...]).astype(o_ref.dtype)


def _attn_sample(page_table, q, knew, cache_ckv, cache_kpe, n_new):
    db, r, dk = q.shape
    n_pages = page_table.shape[1]
    pps = PAGES_PER_STEP
    page = cache_ckv.shape[1]
    newp = knew.shape[1]

    def page_spec(width, j):
        return pl.BlockSpec((None, page, width), lambda b, g, pt: (pt[b, g * pps + j], 0, 0))

    grid_spec = pltpu.PrefetchScalarGridSpec(
        num_scalar_prefetch=1,
        grid=(db, n_pages // pps),
        in_specs=[pl.BlockSpec((1, r, dk), lambda b, g, pt: (b, 0, 0)),
                  pl.BlockSpec((1, newp, dk), lambda b, g, pt: (b, 0, 0))]
                 + [page_spec(KV_RANK, j) for j in range(pps)]
                 + [page_spec(QK_ROPE, j) for j in range(pps)],
        out_specs=pl.BlockSpec((1, r, KV_RANK), lambda b, g, pt: (b, 0, 0)),
        scratch_shapes=[pltpu.VMEM((r, 1), F32), pltpu.VMEM((r, 1), F32), pltpu.VMEM((r, KV_RANK), F32)],
    )
    return pl.pallas_call(
        functools.partial(_attn_sample_kernel, n_new=n_new),
        grid_spec=grid_spec,
        out_shape=jax.ShapeDtypeStruct((db, r, KV_RANK), BF16),
        compiler_params=_cparams(("parallel", "arbitrary")),
        name="attn_sample",
    )(page_table, q, knew, *([cache_ckv] * pps), *([cache_kpe] * pps))


def _gate_prep_kernel(g_ref, bias_ref, valid_ref, triu_ref, o_ref):
    x = g_ref[...] + bias_ref[...]
    valid = valid_ref[...] > 0.5
    is_i = lax.broadcasted_iota(jnp.int32, x.shape, 1) < ML_HEADS
    lf = jnp.minimum(x, 0.0) - jnp.log1p(jnp.exp(-jnp.abs(x)))
    y = jnp.where(is_i, jnp.where(valid, x, NEG_BIG), jnp.where(valid, lf, 0.0))
    r, g8, l = x.shape
    cum = _dot_exact_rhs01(y.reshape(r * g8, l), triu_ref[...]).reshape(r, g8, l)
    o_ref[...] = jnp.where(is_i, y, cum)


def _gate_prep(gt, bias, valid):
    r, g8, l = gt.shape
    rb = 8 if r % 8 == 0 else r
    triu = (jnp.arange(l)[:, None] <= jnp.arange(l)[None, :]).astype(BF16)
    return pl.pallas_call(
        _gate_prep_kernel,
        grid=(r // rb,),
        in_specs=[pl.BlockSpec((rb, g8, l), lambda i: (i, 0, 0)), _full(bias.shape),
                  pl.BlockSpec((rb, 1, l), lambda i: (i, 0, 0)), _full(triu.shape)],
        out_specs=pl.BlockSpec((rb, g8, l), lambda i: (i, 0, 0)),
        out_shape=jax.ShapeDtypeStruct((r, g8, l), F32),
        compiler_params=_cparams(("parallel",)),
        name="gate_prep",
    )(gt, bias, valid, triu)


def _mlstm_kernel(q_ref, k_ref, v_ref, rf_ref, cf_ref, c0_ref, n0_ref, m0_ref,
                  h_ref, c_ref, n_ref, m_ref, c_sc, n_sc, m_sc):
    n_chunks = rf_ref.shape[1]
    lc = ML_CHUNK
    c_sc[...] = c0_ref[0]
    n_sc[...] = n0_ref[0]
    m_sc[...] = m0_ref[0]
    row = lax.broadcasted_iota(jnp.int32, (lc, lc), 0)
    colm = lax.broadcasted_iota(jnp.int32, (lc, lc), 1)
    causal = row >= colm

    def body(ci, carry):
        start = pl.multiple_of(ci * lc, lc)
        rf = rf_ref[0, ci]
        cf = cf_ref[0, ci]
        for hd in range(ML_HEADS):
            sl = slice(hd * ML_DH, (hd + 1) * ML_DH)
            q = q_ref[0, pl.ds(start, lc), sl]
            k = k_ref[0, pl.ds(start, lc), sl]
            v = v_ref[0, pl.ds(start, lc), sl]
            ig_row, b_row = rf[hd:hd + 1, :], rf[ML_HEADS + hd:ML_HEADS + hd + 1, :]
            ig_col, b_col = cf[:, hd:hd + 1], cf[:, ML_HEADS + hd:ML_HEADS + hd + 1]
            c0 = c_sc[hd]
            n0 = n_sc[hd:hd + 1, :]
            m0 = m_sc[0:1, hd:hd + 1]
            d = jnp.where(causal, b_col - b_row + ig_row, -jnp.inf)
            inter = b_col + m0
            m = jnp.maximum(inter, jnp.max(d, axis=1, keepdims=True))
            w = jnp.exp(d - m)
            a = jnp.exp(inter - m)
            wqk = w * _dot_nt(q, k)
            num = a * _dot_nt(q, c0.astype(BF16)) + _dot(wqk.astype(BF16), v)
            qf = q.astype(F32)
            den = a * jnp.sum(qf * n0, axis=1, keepdims=True) + jnp.sum(wqk, axis=1, keepdims=True)
            h_ref[0, pl.ds(start, lc), sl] = num / jnp.maximum(jnp.abs(den), jnp.exp(-m))
            b_last = b_col[lc - 1:lc, :]
            dec_row = b_last - b_row + ig_row
            m_last = jnp.maximum(b_last + m0, jnp.max(dec_row, axis=1, keepdims=True))
            ws_col = jnp.exp(b_last - b_col + ig_col - m_last)
            a_last = jnp.exp(b_last + m0 - m_last)
            c_sc[hd] = a_last * c0 + _dot_tn((v.astype(F32) * ws_col).astype(BF16), k)
            n_sc[hd:hd + 1, :] = a_last * n0 + jnp.sum(ws_col * k.astype(F32), axis=0, keepdims=True)
            m_sc[0:1, hd:hd + 1] = m_last
        return carry

    lax.fori_loop(0, n_chunks, body, 0)
    c_ref[0] = c_sc[...]
    n_ref[0] = n_sc[...]
    m_ref[0] = m_sc[...]


def _mlstm(q, k, v, rf, cf, c0, n0, m0):
    b, t, w = q.shape
    nc = rf.shape[1]
    seq = pl.BlockSpec((1, t, w), lambda i: (i, 0, 0))
    st = [pl.BlockSpec((1,) + c0.shape[1:], lambda i: (i, 0, 0, 0)),
          pl.BlockSpec((1,) + n0.shape[1:], lambda i: (i, 0, 0)),
          pl.BlockSpec((1,) + m0.shape[1:], lambda i: (i, 0, 0))]
    return pl.pallas_call(
        _mlstm_kernel,
        grid=(b,),
        in_specs=[seq, seq, seq,
                  pl.BlockSpec((1,) + rf.shape[1:], lambda i: (i, 0, 0, 0)),
                  pl.BlockSpec((1,) + cf.shape[1:], lambda i: (i, 0, 0, 0))] + st,
        out_specs=[seq] + st,
        out_shape=[jax.ShapeDtypeStruct((b, t, w), F32), jax.ShapeDtypeStruct(c0.shape, F32),
                   jax.ShapeDtypeStruct(n0.shape, F32), jax.ShapeDtypeStruct(m0.shape, F32)],
        scratch_shapes=[pltpu.VMEM(c0.shape[1:], F32), pltpu.VMEM(n0.shape[1:], F32), pltpu.VMEM(m0.shape[1:], F32)],
        compiler_params=_cparams(("parallel",)),
        name="mlstm",
    )(q, k, v, rf, cf, c0, n0, m0)


def _mix_kernel(olat_ref, hml_ref, og_ref, res_ref, wuv_ref, mhg_ref, wo_ref, g_ref, b_ref, o_ref, *, alpha):
    omla = _dot(olat_ref[...], wuv_ref[...])
    hml = hml_ref[...]
    parts = []
    for hd in range(ML_HEADS):
        x = hml[:, hd * ML_DH:(hd + 1) * ML_DH]
        mu = jnp.mean(x, axis=-1, keepdims=True)
        xc = x - mu
        var = jnp.mean(xc * xc, axis=-1, keepdims=True)
        parts.append(xc * lax.rsqrt(var + EPS))
    oml = (jnp.concatenate(parts, axis=1) * mhg_ref[...]) * og_ref[...]
    cat = jnp.concatenate([omla.astype(BF16), oml.astype(BF16)], axis=1)
    a = _dot(cat, wo_ref[...])
    o_ref[...] = _layer_norm(alpha * res_ref[...] + a, g_ref[...], b_ref[...])


def _mix(olat, hml, og, res, wuv, mhg, wo, g, b, alpha):
    n, d = res.shape
    tm = ROW_BLOCK
    rows = lambda w: pl.BlockSpec((tm, w), lambda i: (i, 0))
    return pl.pallas_call(
        functools.partial(_mix_kernel, alpha=alpha),
        grid=(n // tm,),
        in_specs=[rows(olat.shape[1]), rows(ML_W), rows(ML_W), rows(d), _full(wuv.shape), _full(mhg.shape),
                  _full(wo.shape), _full(g.shape), _full(b.shape)],
        out_specs=rows(d),
        out_shape=jax.ShapeDtypeStruct((n, d), F32),
        compiler_params=_cparams(("parallel",)),
        name="mix",
    )(olat, hml, og, res, wuv, mhg, wo, g, b)


def _res_ln_kernel(res_ref, y_ref, g_ref, b_ref, o_ref, *, alpha):
    o_ref[...] = _layer_norm(alpha * res_ref[...] + y_ref[...], g_ref[...], b_ref[...])


def _res_ln(res, y, g, b, alpha):
    n, d = res.shape
    tm = ROW_BLOCK
    rows = pl.BlockSpec((tm, d), lambda i: (i, 0))
    return pl.pallas_call(
        functools.partial(_res_ln_kernel, alpha=alpha),
        grid=(n // tm,),
        in_specs=[rows, rows, _full(g.shape), _full(b.shape)],
        out_specs=rows,
        out_shape=jax.ShapeDtypeStruct((n, d), F32),
        compiler_params=_cparams(("parallel",)),
        name="res_ln",
    )(res, y, g, b)


def _topk_rows(s_sc, n_rows, vals_sc, idx_sc):
    iota = lax.broadcasted_iota(jnp.int32, (n_rows, s_sc.shape[1]), 0).astype(F32)

    def body(kk, carry):
        s = s_sc[0:n_rows, :]
        m = jnp.max(s, axis=0, keepdims=True)
        i = jnp.min(jnp.where(s == m, iota, float(n_rows)), axis=0, keepdims=True)
        vals_sc[pl.ds(kk, 1), :] = m
        idx_sc[pl.ds(kk, 1), :] = i
        s_sc[0:n_rows, :] = jnp.where(iota == i, -jnp.inf, s)
        return carry

    lax.fori_loop(0, PEER_TOPK, body, 0)


def _route_kernel(x_ref, whi_ref, wlo_ref, khi_ref, klo_ref, idx_ref, g_ref,
                  q_sc, s_sc, sv0, si0, sv1, si1, cv, ci):
    hd = pl.program_id(1)
    kq = PEER_DQ // 2

    @pl.when(hd == 0)
    def _():
        x = x_ref[...]
        xh = x.astype(BF16)
        xl = (x - xh.astype(F32)).astype(BF16)
        q = _dot(xh, whi_ref[...]) + _dot(xh, wlo_ref[...]) + _dot(xl, whi_ref[...])
        for j in range(PEER_HEADS):
            q_sc[j] = q[:, j * PEER_DQ:(j + 1) * PEER_DQ]

    qh_all = q_sc[hd]
    for p, (sv, si) in enumerate(((sv0, si0), (sv1, si1))):
        qq = qh_all[:, p * kq:(p + 1) * kq]
        qh = qq.astype(BF16)
        ql = (qq - qh.astype(F32)).astype(BF16)
        kh, kl = khi_ref[0, p], klo_ref[0, p]
        s_sc[0:N_KEYS, :] = _dot_nt(kh, qh) + _dot_nt(kh, ql) + _dot_nt(kl, qh)
        _topk_rows(s_sc, N_KEYS, sv, si)
    v1 = sv1[...]
    for i in range(PEER_TOPK):
        s_sc[i * PEER_TOPK:(i + 1) * PEER_TOPK, :] = sv0[i:i + 1, :] + v1
    _topk_rows(s_sc, PEER_TOPK * PEER_TOPK, cv, ci)
    cidx = ci[...]
    i1 = jnp.floor(cidx * (1.0 / PEER_TOPK))
    i2 = cidx - i1 * PEER_TOPK
    e1 = jnp.zeros_like(cidx)
    e2 = jnp.zeros_like(cidx)
    for i in range(PEER_TOPK):
        e1 = jnp.where(i1 == float(i), si0[i:i + 1, :], e1)
        e2 = jnp.where(i2 == float(i), si1[i:i + 1, :], e2)
    idx_ref[0] = (e1 * float(N_KEYS) + e2).astype(jnp.int32)
    cvv = cv[...]
    e = jnp.exp(cvv - cvv[0:1, :])
    g_ref[0] = e / jnp.sum(e, axis=0, keepdims=True)


def _route(x, whi, wlo, khi, klo):
    n, d = x.shape
    tm = ROW_BLOCK
    out_spec = pl.BlockSpec((1, PEER_TOPK, tm), lambda i, h: (h, 0, i))
    key_spec = pl.BlockSpec((1, 2, N_KEYS, PEER_DQ // 2), lambda i, h: (h, 0, 0, 0))
    tk = pltpu.VMEM((PEER_TOPK, tm), F32)
    return pl.pallas_call(
        _route_kernel,
        grid=(n // tm, PEER_HEADS),
        in_specs=[pl.BlockSpec((tm, d), lambda i, h: (i, 0)), _full(whi.shape), _full(wlo.shape), key_spec, key_spec],
        out_specs=[out_spec, out_spec],
        out_shape=[jax.ShapeDtypeStruct((PEER_HEADS, PEER_TOPK, n), jnp.int32),
                   jax.ShapeDtypeStruct((PEER_HEADS, PEER_TOPK, n), F32)],
        scratch_shapes=[pltpu.VMEM((PEER_HEADS, tm, PEER_DQ), F32), pltpu.VMEM((PEER_TOPK * PEER_TOPK, tm), F32),
                        tk, tk, tk, tk, tk, tk],
        compiler_params=_cparams(("parallel", "arbitrary")),
        name="peer_route",
    )(x, whi, wlo, khi, klo)


ROW_WORDS = 4
SLOTS = PEER_HEADS * PEER_TOPK


def _gather_rows(idx_ref, tab_ref, g_sc, t):
    for r in range(SLOTS):
        e4 = pl.multiple_of(idx_ref[t * SLOTS + r], ROW_WORDS)
        g_sc[pl.ds(ROW_WORDS * r, ROW_WORDS), :] = tab_ref[pl.ds(e4, ROW_WORDS), :]
    return pltpu.bitcast(g_sc[...], BF16)


def _peer_u_kernel(idx_ref, x_ref, tab_ref, sel_ref, act_ref, g_sc, d_sc):
    tb = x_ref.shape[0]
    rows16 = 2 * SUBLANES
    sub = lax.broadcasted_iota(jnp.int32, (rows16, SUBLANES * SLOTS), 0)
    lane = lax.broadcasted_iota(jnp.int32, (rows16, SUBLANES * SLOTS), 1)
    diag = _imod(lane, SUBLANES) == _imod(sub, SUBLANES)

    def body(t, carry):
        gb = _gather_rows(idx_ref, tab_ref, g_sc, t)
        d = _dot_nt(x_ref[t], gb)
        d_sc[pl.ds(t, 1), :] = jnp.sum(jnp.where(diag, d, 0.0), axis=0, keepdims=True)
        return carry

    lax.fori_loop(0, tb, body, 0)
    act_ref[...] = _dot_exact_rhs01(d_sc[...], sel_ref[...])


def _peer_v_kernel(idx_ref, act_ref, gate_ref, tab_ref, selt_ref, o_ref, g_sc, w_sc):
    tb = act_ref.shape[0]
    rows16 = 2 * SUBLANES
    w = gate_ref[...] * jax.nn.gelu(act_ref[...], approximate=True)
    w_sc[...] = _dot_exact_rhs01(w, selt_ref[...])
    sub = lax.broadcasted_iota(jnp.int32, (rows16, SUBLANES * SLOTS), 0)
    lane = lax.broadcasted_iota(jnp.int32, (rows16, SUBLANES * SLOTS), 1)
    diag = _imod(lane, SUBLANES) == _imod(sub, SUBLANES)
    top = sub < SUBLANES

    def body(t, carry):
        gb = _gather_rows(idx_ref, tab_ref, g_sc, t)
        w16 = jnp.where(diag, jnp.broadcast_to(w_sc[pl.ds(t, 1), :], diag.shape), 0.0)
        hi = w16.astype(BF16).astype(F32)
        lhs = jnp.where(top, hi, w16 - hi).astype(BF16)
        o16 = _dot(lhs, gb)
        o_ref[pl.ds(pl.multiple_of(t * SUBLANES, SUBLANES), SUBLANES), :] = o16[0:SUBLANES] + o16[SUBLANES:]
        return carry

    lax.fori_loop(0, tb, body, 0)


def _table_spec(shape):
    return pl.BlockSpec(shape, lambda i: (0, 0), pipeline_mode=pl.Buffered(1))


def _peer_u(idx4, x16, tab, sel):
    n = x16.shape[0]
    tb = PEER_TB
    return pl.pallas_call(
        _peer_u_kernel,
        grid=(n // tb,),
        in_specs=[pl.BlockSpec((tb * SLOTS,), lambda i: (i,), memory_space=pltpu.SMEM),
                  pl.BlockSpec((tb, 2 * SUBLANES, LANES), lambda i: (i, 0, 0)),
                  _table_spec(tab.shape), _full(sel.shape)],
        out_specs=pl.BlockSpec((tb, SLOTS), lambda i: (i, 0)),
        out_shape=jax.ShapeDtypeStruct((n, SLOTS), F32),
        scratch_shapes=[pltpu.VMEM((ROW_WORDS * SLOTS, LANES), jnp.uint32), pltpu.VMEM((tb, SUBLANES * SLOTS), F32)],
        compiler_params=_cparams(("parallel",), vmem=56 * 1024 * 1024),
        name="peer_u",
    )(idx4, x16, tab, sel)


def _peer_v(idx4, act, gate, tab, selt):
    n = act.shape[0]
    tb = PEER_TB
    return pl.pallas_call(
        _peer_v_kernel,
        grid=(n // tb,),
        in_specs=[pl.BlockSpec((tb * SLOTS,), lambda i: (i,), memory_space=pltpu.SMEM),
                  pl.BlockSpec((tb, SLOTS), lambda i: (i, 0)), pl.BlockSpec((tb, SLOTS), lambda i: (i, 0)),
                  _table_spec(tab.shape), _full(selt.shape)],
        out_specs=pl.BlockSpec((tb * SUBLANES, LANES), lambda i: (i, 0)),
        out_shape=jax.ShapeDtypeStruct((n * SUBLANES, LANES), F32),
        scratch_shapes=[pltpu.VMEM((ROW_WORDS * SLOTS, LANES), jnp.uint32), pltpu.VMEM((tb, SUBLANES * SLOTS), F32)],
        compiler_params=_cparams(("parallel",), vmem=56 * 1024 * 1024),
        name="peer_v",
    )(idx4, act, gate, tab, selt)


def _pack_table(t):
    e, d = t.shape
    pair = t.astype(BF16).reshape(e, 2, d // 2).transpose(0, 2, 1)
    return lax.bitcast_convert_type(pair, jnp.uint32).reshape(e * ROW_WORDS, LANES)


def _to_chunk_order(x):
    n, d = x.shape
    return x.reshape(n, 2, ROW_WORDS, LANES).transpose(0, 2, 1, 3).reshape(n, 2 * ROW_WORDS, LANES)


def _from_chunk_order(y, n):
    return y.reshape(n, ROW_WORDS, 2, LANES).transpose(0, 2, 1, 3).reshape(n, 2 * ROW_WORDS * LANES)


def _peer(x, whi, wlo, khi, klo, tab_u, tab_v):
    n = x.shape[0]
    idx_t, gate_t = _route(x, whi, wlo, khi, klo)
    idx4 = (idx_t.reshape(SLOTS, n).T * ROW_WORDS).reshape(n * SLOTS)
    gate = gate_t.reshape(SLOTS, n).T
    xc = _to_chunk_order(x)
    xh = xc.astype(BF16)
    xl = (xc - xh.astype(F32)).astype(BF16)
    x16 = jnp.concatenate([xh, xl], axis=1)
    grp = jnp.arange(SUBLANES * SLOTS) // SUBLANES
    sel = (grp[:, None] == jnp.arange(SLOTS)[None, :]).astype(BF16)
    act = _peer_u(idx4, x16, tab_u, sel)
    y = _peer_v(idx4, act, gate, tab_v, sel.T)
    return _from_chunk_order(y, n)


def _pad_rows(x, n):
    return jnp.pad(x, ((0, n - x.shape[0]),) + ((0, 0),) * (x.ndim - 1))


def _rope_tables(pos):
    half = QK_ROPE // 2
    inv = ROPE_THETA ** (-jnp.arange(half, dtype=F32) / half)
    ang = pos.astype(F32)[:, None] * inv
    reps = LANES // half
    return jnp.tile(jnp.cos(ang), (1, reps)), jnp.tile(jnp.sin(ang), (1, reps))


def _layer_weights(w_in, w_qb, w_kvb):
    d = w_in.shape[0]
    o_ckv = Q_RANK
    o_kr = o_ckv + KV_RANK
    o_mq = o_kr + QK_ROPE
    o_g = o_mq + 4 * ML_W
    small = jnp.concatenate([w_in[:, o_kr:o_mq], w_in[:, o_g:o_g + 2 * ML_HEADS],
                             jnp.zeros((d, LANES - QK_ROPE - 2 * ML_HEADS), w_in.dtype)], axis=1)
    win = jnp.concatenate([w_in[:, :o_kr], small, w_in[:, o_mq:o_g]], axis=1).astype(BF16)
    half = QK_ROPE // 2
    wq = w_qb.reshape(Q_RANK, MLA_HEADS, QK_NOPE + QK_ROPE)
    wqb = jnp.concatenate([wq[..., :QK_NOPE].reshape(Q_RANK, -1), wq[..., QK_NOPE:QK_NOPE + half].reshape(Q_RANK, -1),
                           wq[..., QK_NOPE + half:].reshape(Q_RANK, -1)], axis=1).astype(BF16)
    wkv = w_kvb.reshape(KV_RANK, MLA_HEADS, QK_NOPE + V_DIM)
    eye = jnp.eye(MLA_HEADS, dtype=w_kvb.dtype)
    wuk = jnp.einsum('chd,hg->hdgc', wkv[..., :QK_NOPE], eye).reshape(MLA_HEADS * QK_NOPE, MLA_HEADS * KV_RANK)
    wuv = jnp.einsum('chv,hg->hcgv', wkv[..., QK_NOPE:], eye).reshape(MLA_HEADS * KV_RANK, MLA_W)
    return win, wqb, wuk.astype(BF16), wuv.astype(BF16)


def _hi_lo(w):
    hi = w.astype(BF16)
    return hi, (w - hi.astype(F32)).astype(BF16)


def kernel(x_prompt, x_sample, cache_ckv, cache_kpe, page_table, state_C, state_n, state_m, meta_tokens,
           ln_in_g, ln_in_b, w_in, b_gates, q_norm_g, w_qb, kv_norm_g, w_kvb, mh_norm_g, w_o, ln1_g, ln1_b,
           w_pq, sub_keys, expert_u, expert_v, ln2_g, ln2_b):
    B, seq, D = x_prompt.shape
    DB, S = x_sample.shape[:2]
    depth = w_in.shape[0]
    T = seq + N_META
    page = cache_ckv.shape[2]
    past_len = page_table.shape[1] * page
    alpha = (2.0 * depth) ** 0.25
    n_p, n_s = B * T, DB * S
    n_rows = n_p + n_s
    n_pad = -(-n_rows // ROW_BLOCK) * ROW_BLOCK
    row = lambda a: a.reshape(1, -1)

    meta = jnp.broadcast_to(meta_tokens[None].astype(x_prompt.dtype), (B, N_META, D))
    x = jnp.concatenate([jnp.concatenate([meta, x_prompt], axis=1).reshape(n_p, D), x_sample.reshape(n_s, D)], axis=0)
    x = _pad_rows(x, n_pad)
    pos = jnp.concatenate([jnp.tile(jnp.arange(T), B), jnp.tile(past_len + jnp.arange(S), DB),
                           jnp.zeros((n_pad - n_rows,), jnp.int32)])
    cos_t, sin_t = _rope_tables(pos)

    t_att = -(-T // ATT_TK) * ATT_TK
    nc_p = -(-T // ML_CHUNK)
    t_ml = nc_p * ML_CHUNK
    front = t_ml - T
    s_pad = ML_CHUNK
    new_pad = 16

    outs = {k: [] for k in ("ckv_p", "kpe_p", "C_p", "n_p", "m_p", "ckv_s", "kpe_s", "C_s", "n_s", "m_s")}
    h = x
    for l in range(depth):
        win, wqb, wuk, wuv = _layer_weights(w_in[l], w_qb[l], w_kvb[l])
        (h, ckv, kpe, zs, qlat, qrope, mq, mk, mv, og) = _proj(
            h, cos_t, sin_t, row(ln_in_g), row(ln_in_b), win, row(q_norm_g[l]), wqb, wuk, row(kv_norm_g[l]),
            apply_ln=(l == 0))
        gates = zs[:, QK_ROPE:QK_ROPE + 2 * ML_HEADS]
        bias = b_gates[l].astype(F32).reshape(1, 2 * ML_HEADS, 1)

        half = QK_ROPE // 2
        qcat = jnp.concatenate([qlat.reshape(n_pad, MLA_HEADS, KV_RANK),
                                qrope[:, :LANES].reshape(n_pad, MLA_HEADS, half),
                                qrope[:, LANES:].reshape(n_pad, MLA_HEADS, half)], axis=-1)
        kcat = jnp.concatenate([ckv, kpe], axis=-1).astype(BF16)
        dk = KV_RANK + QK_ROPE

        q_p = jnp.pad(qcat[:n_p].reshape(B, T, MLA_HEADS, dk), ((0, 0), (0, t_att - T), (0, 0), (0, 0)))
        k_p = jnp.pad(kcat[:n_p].reshape(B, T, dk), ((0, 0), (0, t_att - T), (0, 0)))
        o_p = _attn_prompt(q_p.transpose(0, 2, 1, 3), k_p)
        olat_p = o_p[:, :, :T].transpose(0, 2, 1, 3).reshape(n_p, MLA_HEADS * KV_RANK)

        q_s = qcat[n_p:n_rows].reshape(DB, S, MLA_HEADS, dk).transpose(0, 2, 1, 3).reshape(DB, MLA_HEADS * S, dk)
        k_s = jnp.pad(kcat[n_p:n_rows].reshape(DB, S, dk), ((0, 0), (0, new_pad - S), (0, 0)))
        o_s = _attn_sample(page_table, q_s, k_s, cache_ckv[l], cache_kpe[l], S)
        olat_s = o_s.reshape(DB, MLA_HEADS, S, KV_RANK).transpose(0, 2, 1, 3).reshape(n_s, MLA_HEADS * KV_RANK)
        olat = _pad_rows(jnp.concatenate([olat_p, olat_s], axis=0), n_pad)

        def seqs(a, nb, t, pad):
            return jnp.pad(a.reshape(nb, t, a.shape[-1]), ((0, 0), pad, (0, 0)))

        def gate_forms(g_rows, nb, t, pad, nc):
            gp = seqs(g_rows, nb, t, pad).reshape(nb * nc, ML_CHUNK, 2 * ML_HEADS).transpose(0, 2, 1)
            valid = seqs(jnp.ones((nb * t, 1), F32), nb, t, pad).reshape(nb * nc, 1, ML_CHUNK)
            rf = _gate_prep(gp, bias, valid).reshape(nb, nc, 2 * ML_HEADS, ML_CHUNK)
            return rf, rf.transpose(0, 1, 3, 2)

        rf_p, cf_p = gate_forms(gates[:n_p], B, T, (front, 0), nc_p)
        zero_c = jnp.zeros((B, ML_HEADS, ML_DH, ML_DH), F32)
        h_p, C_p, nn_p, m_p = _mlstm(seqs(mq[:n_p], B, T, (front, 0)), seqs(mk[:n_p], B, T, (front, 0)),
                                     seqs(mv[:n_p], B, T, (front, 0)), rf_p, cf_p,
                                     zero_c, jnp.zeros((B, ML_HEADS, ML_DH), F32), jnp.zeros((B, 1, ML_HEADS), F32))
        rf_s, cf_s = gate_forms(gates[n_p:n_rows], DB, S, (0, s_pad - S), 1)
        h_s, C_s, nn_s, m_s = _mlstm(seqs(mq[n_p:n_rows], DB, S, (0, s_pad - S)), seqs(mk[n_p:n_rows], DB, S, (0, s_pad - S)),
                                     seqs(mv[n_p:n_rows], DB, S, (0, s_pad - S)), rf_s, cf_s,
                                     state_C[l].astype(F32), state_n[l].astype(F32),
                                     state_m[l].astype(F32).reshape(DB, 1, ML_HEADS))
        hml = _pad_rows(jnp.concatenate([h_p[:, front:].reshape(n_p, ML_W), h_s[:, :S].reshape(n_s, ML_W)], axis=0), n_pad)

        h1 = _mix(olat, hml, og, h, wuv, row(mh_norm_g[l]), w_o[l].astype(BF16), row(ln1_g[l]), row(ln1_b[l]), alpha)
        whi, wlo = _hi_lo(w_pq[l])
        khi, klo = _hi_lo(sub_keys[l])
        y = _peer(h1, whi, wlo, khi, klo, _pack_table(expert_u[l]), _pack_table(expert_v[l]))
        h = _res_ln(h1, y, row(ln2_g[l]), row(ln2_b[l]), alpha)

        dt = x_prompt.dtype
        outs["ckv_p"].append(ckv[:n_p].reshape(B, T, KV_RANK))
        outs["kpe_p"].append(kpe[:n_p].reshape(B, T, QK_ROPE))
        outs["C_p"].append(C_p.astype(dt)); outs["n_p"].append(nn_p.astype(dt)); outs["m_p"].append(m_p.reshape(B, ML_HEADS).astype(dt))
        outs["ckv_s"].append(ckv[n_p:n_rows].reshape(DB, S, KV_RANK))
        outs["kpe_s"].append(kpe[n_p:n_rows].reshape(DB, S, QK_ROPE))
        outs["C_s"].append(C_s.astype(dt)); outs["n_s"].append(nn_s.astype(dt)); outs["m_s"].append(m_s.reshape(DB, ML_HEADS).astype(dt))

    y_prompt = h[:n_p].reshape(B, T, D)[:, N_META:]
    y_sample = h[n_p:n_rows].reshape(DB, S, D)
    st = lambda k: jnp.stack(outs[k])
    return (y_prompt, y_sample, st("ckv_p"), st("kpe_p"), st("C_p"), st("n_p"), st("m_p"),
            st("ckv_s"), st("kpe_s"), st("C_s"), st("n_s"), st("m_s"))
```

```python
import functools

import jax
import jax.numpy as jnp
from jax import lax
from jax.experimental import pallas as pl
from jax.experimental.pallas import tpu as pltpu

F32 = jnp.float32
BF16 = jnp.bfloat16

N_META = 16
MLA_HEADS = 8
QK_NOPE = 64
QK_ROPE = 32
V_DIM = 64
Q_RANK = 256
KV_RANK = 128
ROPE_THETA = 10000.0
SM_SCALE = (QK_NOPE + QK_ROPE) ** -0.5
ML_HEADS = 4
ML_DH = 128
ML_W = ML_HEADS * ML_DH
MLA_W = MLA_HEADS * V_DIM
PEER_HEADS = 8
N_KEYS = 128
PEER_DQ = 256
PEER_TOPK = 16
EPS = 1e-5

LANES = 128
SUBLANES = 8
ROW_BLOCK = 256
ATT_TQ = 128
ATT_TK = 256
ML_CHUNK = 128
PAGES_PER_STEP = 8
PEER_TB = 128
NEG_BIG = -1e30
VMEM_LIMIT = 48 * 1024 * 1024

ZW = 2560
Z_CQ, Z_CKV, Z_SMALL, Z_MQ, Z_MK, Z_MV, Z_MO = 0, 256, 384, 512, 1024, 1536, 2048


def _dot(a, b):
    return jnp.dot(a, b, preferred_element_type=F32)


def _dot_nt(a, b):
    return lax.dot_general(a, b, (((1,), (1,)), ((), ())), preferred_element_type=F32)


def _dot_tn(a, b):
    return lax.dot_general(a, b, (((0,), (0,)), ((), ())), preferred_element_type=F32)


def _split3(x):
    a = x.astype(BF16)
    r = x - a.astype(F32)
    b = r.astype(BF16)
    c = (r - b.astype(F32)).astype(BF16)
    return a, b, c


def _dot_exact_rhs01(x, sel):
    a, b, c = _split3(x)
    return _dot(a, sel) + _dot(b, sel) + _dot(c, sel)


def _imod(i, n):
    if n & (n - 1) == 0:
        return jnp.bitwise_and(i, n - 1)
    f = i.astype(F32)
    return (f - n * jnp.floor(f / n)).astype(jnp.int32)


def _layer_norm(x, g, b):
    mu = jnp.mean(x, axis=-1, keepdims=True)
    xc = x - mu
    var = jnp.mean(xc * xc, axis=-1, keepdims=True)
    return xc * lax.rsqrt(var + EPS) * g + b


def _rms_norm(x, g):
    return x * lax.rsqrt(jnp.mean(x * x, axis=-1, keepdims=True) + 1e-6) * g


def _cparams(sem, vmem=VMEM_LIMIT):
    return pltpu.CompilerParams(dimension_semantics=sem, vmem_limit_bytes=vmem)


def _full(shape):
    n = len(shape)
    return pl.BlockSpec(shape, lambda *_: (0,) * n)


def _proj_kernel(x_ref, cos_ref, sin_ref, lng_ref, lnb_ref, win_ref, qng_ref, wqb_ref, wuk_ref, kvg_ref,
                 h_ref, ckv_ref, kpe_ref, gates_ref, qlat_ref, qrope_ref, mq_ref, mk_ref, mv_ref, og_ref,
                 *, apply_ln):
    x = x_ref[...]
    h = _layer_norm(x, lng_ref[...], lnb_ref[...]) if apply_ln else x
    h_ref[...] = h
    z = _dot(h.astype(BF16), win_ref[...])
    ckv_ref[...] = _rms_norm(z[:, Z_CKV:Z_SMALL], kvg_ref[...])
    q = _dot(_rms_norm(z[:, Z_CQ:Z_CKV], qng_ref[...]).astype(BF16), wqb_ref[...])
    nope_w = MLA_HEADS * QK_NOPE
    qlat_ref[...] = _dot(q[:, :nope_w].astype(BF16), wuk_ref[...]).astype(BF16)
    c = cos_ref[...]
    s = sin_ref[...]
    x1 = q[:, nope_w:nope_w + LANES]
    x2 = q[:, nope_w + LANES:nope_w + 2 * LANES]
    qrope_ref[:, 0:LANES] = (x1 * c - x2 * s).astype(BF16)
    qrope_ref[:, LANES:2 * LANES] = (x1 * s + x2 * c).astype(BF16)
    zs = z[:, Z_SMALL:Z_MQ]
    half = QK_ROPE // 2
    lane = lax.broadcasted_iota(jnp.int32, zs.shape, 1)
    first = lane < half
    partner = jnp.where(first, pltpu.roll(zs, LANES - half, 1), pltpu.roll(zs, half, 1))
    kpe = zs * c + partner * jnp.where(first, -s, s)
    kpe_ref[...] = kpe[:, 0:QK_ROPE]
    gates_ref[...] = zs
    mq_ref[...] = z[:, Z_MQ:Z_MK].astype(BF16)
    mk_ref[...] = (z[:, Z_MK:Z_MV] * (ML_DH ** -0.5)).astype(BF16)
    mv_ref[...] = z[:, Z_MV:Z_MO].astype(BF16)
    og_ref[...] = jax.nn.sigmoid(z[:, Z_MO:ZW])


def _proj(x, cos_t, sin_t, lng, lnb, win, qng, wqb, wuk, kvg, *, apply_ln):
    n, d = x.shape
    tm = ROW_BLOCK
    rows = lambda w: pl.BlockSpec((tm, w), lambda i: (i, 0))
    out_w = [(d, F32), (KV_RANK, F32), (QK_ROPE, F32), (LANES, F32), (MLA_HEADS * KV_RANK, BF16),
             (2 * LANES, BF16), (ML_W, BF16), (ML_W, BF16), (ML_W, BF16), (ML_W, F32)]
    return pl.pallas_call(
        functools.partial(_proj_kernel, apply_ln=apply_ln),
        grid=(n // tm,),
        in_specs=[rows(d), rows(LANES), rows(LANES), _full(lng.shape), _full(lnb.shape), _full(win.shape),
                  _full(qng.shape), _full(wqb.shape), _full(wuk.shape), _full(kvg.shape)],
        out_specs=[rows(w) for w, _ in out_w],
        out_shape=[jax.ShapeDtypeStruct((n, w), dt) for w, dt in out_w],
        compiler_params=_cparams(("parallel",)),
        name="proj",
    )(x, cos_t, sin_t, lng, lnb, win, qng, wqb, wuk, kvg)


def _attn_prompt_kernel(q_ref, kt_ref, v_ref, o_ref, m_sc, acc_sc):
    i = pl.program_id(1)
    nh, tq, dk = q_ref.shape[1:]
    tk = ATT_TK
    reps = tk // LANES
    m_sc[...] = jnp.full(m_sc.shape, -jnp.inf, F32)
    acc_sc[...] = jnp.zeros(acc_sc.shape, F32)
    q_pos = i * tq + lax.broadcasted_iota(jnp.int32, (tq, tk), 0)
    col = lax.broadcasted_iota(jnp.int32, (tq, tk), 1)
    n_full = (i * tq + 1) // tk

    def chunk(c, masked):
        kt = kt_ref[0, c]
        v = v_ref[0, pl.ds(pl.multiple_of(c * tk, tk), tk), :]
        for hd in range(nh):
            rows = slice(hd * tq, (hd + 1) * tq)
            s = _dot(q_ref[0, hd], kt) * SM_SCALE
            if masked:
                s = jnp.where(col + c * tk <= q_pos, s, -jnp.inf)
            m_prev = m_sc[rows, :]
            m_new = jnp.maximum(m_prev, jnp.max(s, axis=1, keepdims=True))
            alpha = jnp.exp(m_prev - m_new)
            p = jnp.exp(s - jnp.tile(m_new, (1, reps)))
            acc_sc[rows, :] = jnp.tile(alpha, (1, 2)) * acc_sc[rows, :] + _dot(p.astype(BF16), v)
            m_sc[rows, :] = m_new

    def body(c, carry):
        chunk(c, False)
        return carry

    lax.fori_loop(0, n_full, body, 0)
    chunk(n_full, True)
    for hd in range(nh):
        acc = acc_sc[hd * tq:(hd + 1) * tq, :]
        o_ref[0, hd] = (acc[:, :KV_RANK] / acc[:, KV_RANK:]).astype(o_ref.dtype)


def _attn_prompt(q, kt, v):
    b, nh, tp, dk = q.shape
    tq = ATT_TQ
    return pl.pallas_call(
        _attn_prompt_kernel,
        grid=(b, tp // tq),
        in_specs=[pl.BlockSpec((1, nh, tq, dk), lambda bi, i: (bi, 0, i, 0)),
                  pl.BlockSpec((1,) + kt.shape[1:], lambda bi, i: (bi, 0, 0, 0)),
                  pl.BlockSpec((1,) + v.shape[1:], lambda bi, i: (bi, 0, 0))],
        out_specs=pl.BlockSpec((1, nh, tq, KV_RANK), lambda bi, i: (bi, 0, i, 0)),
        out_shape=jax.ShapeDtypeStruct((b, nh, tp, KV_RANK), BF16),
        scratch_shapes=[pltpu.VMEM((nh * tq, LANES), F32), pltpu.VMEM((nh * tq, 2 * KV_RANK), F32)],
        compiler_params=_cparams(("parallel", "arbitrary")),
        name="attn_prompt",
    )(q, kt, v)


def _attn_sample_kernel(pt_ref, q_ref, knew_ref, *refs, n_new):
    pps = PAGES_PER_STEP
    ckv_refs, kpe_refs = refs[:pps], refs[pps:2 * pps]
    o_ref, m_sc, l_sc, acc_sc = refs[2 * pps:]
    g = pl.program_id(1)
    q = q_ref[0]
    ql, qr = q[:, :KV_RANK], q[:, KV_RANK:]

    @pl.when(g == 0)
    def _():
        m_sc[...] = jnp.full(m_sc.shape, -jnp.inf, F32)
        l_sc[...] = jnp.zeros(l_sc.shape, F32)
        acc_sc[...] = jnp.zeros(acc_sc.shape, F32)

    def update(s, vals):
        m_prev = m_sc[...]
        m_new = jnp.maximum(m_prev, jnp.max(s, axis=1, keepdims=True))
        alpha = jnp.exp(m_prev - m_new)
        p = jnp.exp(s - m_new)
        l_sc[...] = alpha * l_sc[...] + jnp.sum(p, axis=1, keepdims=True)
        acc = alpha * acc_sc[...]
        w = p.shape[1] // len(vals)
        for j, v in enumerate(vals):
            acc = acc + _dot(p[:, j * w:(j + 1) * w].astype(BF16), v)
        acc_sc[...] = acc
        m_sc[...] = m_new

    kc = [r[...].astype(BF16) for r in ckv_refs]
    s = jnp.concatenate([_dot_nt(ql, kc[j]) + _dot_nt(qr, kpe_refs[j][...].astype(BF16)) for j in range(pps)],
                        axis=1) * SM_SCALE
    update(s, kc)

    @pl.when(g == pl.num_programs(1) - 1)
    def _():
        kn = knew_ref[0]
        s2 = _dot_nt(q, kn) * SM_SCALE
        row = lax.broadcasted_iota(jnp.int32, s2.shape, 0)
        colk = lax.broadcasted_iota(jnp.int32, s2.shape, 1)
        s2 = jnp.where((colk <= _imod(row, n_new)) & (colk < n_new), s2, -jnp.inf)
        update(s2, [kn[:, :KV_RANK]])
        o_ref[0] = (acc_sc[...] / l_sc[...]).astype(o_ref.dtype)


def _attn_sample(page_table, q, knew, cache_ckv, cache_kpe, n_new):
    db, r, dk = q.shape
    n_pages = page_table.shape[1]
    pps = PAGES_PER_STEP
    page = cache_ckv.shape[1]
    newp = knew.shape[1]

    def page_spec(width, j):
        return pl.BlockSpec((None, page, width), lambda b, g, pt: (pt[b, g * pps + j], 0, 0))

    grid_spec = pltpu.PrefetchScalarGridSpec(
        num_scalar_prefetch=1,
        grid=(db, n_pages // pps),
        in_specs=[pl.BlockSpec((1, r, dk), lambda b, g, pt: (b, 0, 0)),
                  pl.BlockSpec((1, newp, dk), lambda b, g, pt: (b, 0, 0))]
                 + [page_spec(KV_RANK, j) for j in range(pps)]
                 + [page_spec(QK_ROPE, j) for j in range(pps)],
        out_specs=pl.BlockSpec((1, r, KV_RANK), lambda b, g, pt: (b, 0, 0)),
        scratch_shapes=[pltpu.VMEM((r, 1), F32), pltpu.VMEM((r, 1), F32), pltpu.VMEM((r, KV_RANK), F32)],
    )
    return pl.pallas_call(
        functools.partial(_attn_sample_kernel, n_new=n_new),
        grid_spec=grid_spec,
        out_shape=jax.ShapeDtypeStruct((db, r, KV_RANK), BF16),
        compiler_params=_cparams(("parallel", "arbitrary")),
        name="attn_sample",
    )(page_table, q, knew, *([cache_ckv] * pps), *([cache_kpe] * pps))


def _gate_prep_kernel(g_ref, bias_ref, valid_ref, triu_ref, o_ref):
    x = g_ref[...] + bias_ref[...]
    valid = valid_ref[...] > 0.5
    is_i = lax.broadcasted_iota(jnp.int32, x.shape, 1) < ML_HEADS
    lf = jnp.minimum(x, 0.0) - jnp.log1p(jnp.exp(-jnp.abs(x)))
    y = jnp.where(is_i, jnp.where(valid, x, NEG_BIG), jnp.where(valid, lf, 0.0))
    r, g8, l = x.shape
    cum = _dot_exact_rhs01(y.reshape(r * g8, l), triu_ref[...]).reshape(r, g8, l)
    o_ref[...] = jnp.where(is_i, y, cum)


def _gate_prep(gt, bias, valid):
    r, g8, l = gt.shape
    rb = 8 if r % 8 == 0 else r
    triu = (jnp.arange(l)[:, None] <= jnp.arange(l)[None, :]).astype(BF16)
    return pl.pallas_call(
        _gate_prep_kernel,
        grid=(r // rb,),
        in_specs=[pl.BlockSpec((rb, g8, l), lambda i: (i, 0, 0)), _full(bias.shape),
                  pl.BlockSpec((rb, 1, l), lambda i: (i, 0, 0)), _full(triu.shape)],
        out_specs=pl.BlockSpec((rb, g8, l), lambda i: (i, 0, 0)),
        out_shape=jax.ShapeDtypeStruct((r, g8, l), F32),
        compiler_params=_cparams(("parallel",)),
        name="gate_prep",
    )(gt, bias, valid, triu)


def _mlstm_kernel(q_ref, k_ref, v_ref, rf_ref, cf_ref, c0_ref, n0_ref, m0_ref,
                  h_ref, c_ref, n_ref, m_ref, c_sc, n_sc, m_sc):
    n_chunks = rf_ref.shape[1]
    lc = ML_CHUNK
    c_sc[...] = c0_ref[0]
    n_sc[...] = n0_ref[0]
    m_sc[...] = m0_ref[0]
    row = lax.broadcasted_iota(jnp.int32, (lc, lc), 0)
    colm = lax.broadcasted_iota(jnp.int32, (lc, lc), 1)
    causal = row >= colm

    def body(ci, carry):
        start = pl.multiple_of(ci * lc, lc)
        rf = rf_ref[0, ci]
        cf = cf_ref[0, ci]
        for hd in range(ML_HEADS):
            sl = slice(hd * ML_DH, (hd + 1) * ML_DH)
            q = q_ref[0, pl.ds(start, lc), sl]
            k = k_ref[0, pl.ds(start, lc), sl]
            v = v_ref[0, pl.ds(start, lc), sl]
            ig_row, b_row = rf[hd:hd + 1, :], rf[ML_HEADS + hd:ML_HEADS + hd + 1, :]
            ig_col, b_col = cf[:, hd:hd + 1], cf[:, ML_HEADS + hd:ML_HEADS + hd + 1]
            c0 = c_sc[hd]
            n0 = n_sc[hd:hd + 1, :]
            m0 = m_sc[0:1, hd:hd + 1]
            d = jnp.where(causal, b_col - b_row + ig_row, -jnp.inf)
            inter = b_col + m0
            m = jnp.maximum(inter, jnp.max(d, axis=1, keepdims=True))
            w = jnp.exp(d - m)
            a = jnp.exp(inter - m)
            wqk = w * _dot_nt(q, k)
            num = a * _dot_nt(q, c0.astype(BF16)) + _dot(wqk.astype(BF16), v)
            qf = q.astype(F32)
            den = a * jnp.sum(qf * n0, axis=1, keepdims=True) + jnp.sum(wqk, axis=1, keepdims=True)
            h_ref[0, pl.ds(start, lc), sl] = num / jnp.maximum(jnp.abs(den), jnp.exp(-m))
            b_last = b_col[lc - 1:lc, :]
            dec_row = b_last - b_row + ig_row
            m_last = jnp.maximum(b_last + m0, jnp.max(dec_row, axis=1, keepdims=True))
            ws_col = jnp.exp(b_last - b_col + ig_col - m_last)
            a_last = jnp.exp(b_last + m0 - m_last)
            c_sc[hd] = a_last * c0 + _dot_tn((v.astype(F32) * ws_col).astype(BF16), k)
            n_sc[hd:hd + 1, :] = a_last * n0 + jnp.sum(ws_col * k.astype(F32), axis=0, keepdims=True)
            m_sc[0:1, hd:hd + 1] = m_last
        return carry

    lax.fori_loop(0, n_chunks, body, 0)
    c_ref[0] = c_sc[...]
    n_ref[0] = n_sc[...]
    m_ref[0] = m_sc[...]


def _mlstm(q, k, v, rf, cf, c0, n0, m0):
    b, t, w = q.shape
    nc = rf.shape[1]
    seq = pl.BlockSpec((1, t, w), lambda i: (i, 0, 0))
    st = [pl.BlockSpec((1,) + c0.shape[1:], lambda i: (i, 0, 0, 0)),
          pl.BlockSpec((1,) + n0.shape[1:], lambda i: (i, 0, 0)),
          pl.BlockSpec((1,) + m0.shape[1:], lambda i: (i, 0, 0))]
    return pl.pallas_call(
        _mlstm_kernel,
        grid=(b,),
        in_specs=[seq, seq, seq,
                  pl.BlockSpec((1,) + rf.shape[1:], lambda i: (i, 0, 0, 0)),
                  pl.BlockSpec((1,) + cf.shape[1:], lambda i: (i, 0, 0, 0))] + st,
        out_specs=[seq] + st,
        out_shape=[jax.ShapeDtypeStruct((b, t, w), F32), jax.ShapeDtypeStruct(c0.shape, F32),
                   jax.ShapeDtypeStruct(n0.shape, F32), jax.ShapeDtypeStruct(m0.shape, F32)],
        scratch_shapes=[pltpu.VMEM(c0.shape[1:], F32), pltpu.VMEM(n0.shape[1:], F32), pltpu.VMEM(m0.shape[1:], F32)],
        compiler_params=_cparams(("parallel",)),
        name="mlstm",
    )(q, k, v, rf, cf, c0, n0, m0)


def _mix_kernel(olat_ref, hml_ref, og_ref, res_ref, wuv_ref, mhg_ref, wo_ref, g_ref, b_ref, o_ref, *, alpha):
    omla = _dot(olat_ref[...], wuv_ref[...])
    hml = hml_ref[...]
    parts = []
    for hd in range(ML_HEADS):
        x = hml[:, hd * ML_DH:(hd + 1) * ML_DH]
        mu = jnp.mean(x, axis=-1, keepdims=True)
        xc = x - mu
        var = jnp.mean(xc * xc, axis=-1, keepdims=True)
        parts.append(xc * lax.rsqrt(var + EPS))
    oml = (jnp.concatenate(parts, axis=1) * mhg_ref[...]) * og_ref[...]
    cat = jnp.concatenate([omla.astype(BF16), oml.astype(BF16)], axis=1)
    a = _dot(cat, wo_ref[...])
    o_ref[...] = _layer_norm(alpha * res_ref[...] + a, g_ref[...], b_ref[...])


def _mix(olat, hml, og, res, wuv, mhg, wo, g, b, alpha):
    n, d = res.shape
    tm = ROW_BLOCK
    rows = lambda w: pl.BlockSpec((tm, w), lambda i: (i, 0))
    return pl.pallas_call(
        functools.partial(_mix_kernel, alpha=alpha),
        grid=(n // tm,),
        in_specs=[rows(olat.shape[1]), rows(ML_W), rows(ML_W), rows(d), _full(wuv.shape), _full(mhg.shape),
                  _full(wo.shape), _full(g.shape), _full(b.shape)],
        out_specs=rows(d),
        out_shape=jax.ShapeDtypeStruct((n, d), F32),
        compiler_params=_cparams(("parallel",)),
        name="mix",
    )(olat, hml, og, res, wuv, mhg, wo, g, b)


def _res_ln_kernel(res_ref, y_ref, g_ref, b_ref, o_ref, *, alpha):
    o_ref[...] = _layer_norm(alpha * res_ref[...] + y_ref[...], g_ref[...], b_ref[...])


def _res_ln(res, y, g, b, alpha):
    n, d = res.shape
    tm = ROW_BLOCK
    rows = pl.BlockSpec((tm, d), lambda i: (i, 0))
    return pl.pallas_call(
        functools.partial(_res_ln_kernel, alpha=alpha),
        grid=(n // tm,),
        in_specs=[rows, rows, _full(g.shape), _full(b.shape)],
        out_specs=rows,
        out_shape=jax.ShapeDtypeStruct((n, d), F32),
        compiler_params=_cparams(("parallel",)),
        name="res_ln",
    )(res, y, g, b)


def _topk_rounds(jobs):
    def body(kk, carry):
        for s_sc, n_rows, tie, vals_sc, idx_sc in jobs:
            s = s_sc[0:n_rows, :]
            m = jnp.max(s, axis=0, keepdims=True)
            i = jnp.min(jnp.where(s == m, tie, jnp.inf), axis=0, keepdims=True)
            vals_sc[pl.ds(kk, 1), :] = m
            idx_sc[pl.ds(kk, 1), :] = i
            s_sc[0:n_rows, :] = jnp.where(tie == i, -jnp.inf, s)
        return carry

    lax.fori_loop(0, PEER_TOPK, body, 0)


def _candidate_blocks():
    k, sub = PEER_TOPK, SUBLANES
    need = [k // (i + 1) for i in range(k)]
    blocks, covered = [], set()
    for i in range(k):
        if need[i] >= sub:
            nj = -(-need[i] // sub) * sub
            blocks.append(("row", i, nj))
            covered |= {(i, j) for j in range(nj)}
    for i0 in range(0, k, sub):
        rest = [i for i in range(i0, i0 + sub) if need[i] < sub]
        if rest:
            for j in range(max(need[i] for i in rest)):
                blocks.append(("col", j, i0, rest[0]))
                covered |= {(i, j) for i in range(rest[0], i0 + sub)}
    assert all((i, j) in covered for i in range(k) for j in range(k) if (i + 1) * (j + 1) <= k)
    return blocks


CAND_BLOCKS = _candidate_blocks()
CAND_ROWS = sum(b[2] if b[0] == "row" else SUBLANES for b in CAND_BLOCKS)


def _route_kernel(x_ref, whi_ref, wlo_ref, khi_ref, klo_ref, idx_ref, g_ref,
                  q_sc, s0_sc, s1_sc, c_sc, sv0, si0, sv1, si1, cv, ci):
    hd = pl.program_id(1)
    kq = PEER_DQ // 2
    tm = x_ref.shape[0]

    @pl.when(hd == 0)
    def _():
        x = x_ref[...]
        xh = x.astype(BF16)
        xl = (x - xh.astype(F32)).astype(BF16)
        q = _dot(xh, whi_ref[...]) + _dot(xh, wlo_ref[...]) + _dot(xl, whi_ref[...])
        for j in range(PEER_HEADS):
            q_sc[j] = q[:, j * PEER_DQ:(j + 1) * PEER_DQ]

    qh_all = q_sc[hd]
    for p, s_sc in enumerate((s0_sc, s1_sc)):
        qq = qh_all[:, p * kq:(p + 1) * kq]
        qh = qq.astype(BF16)
        ql = (qq - qh.astype(F32)).astype(BF16)
        kh, kl = khi_ref[0, p], klo_ref[0, p]
        s_sc[...] = _dot_nt(kh, qh) + _dot_nt(kh, ql) + _dot_nt(kl, qh)
    key_iota = lax.broadcasted_iota(jnp.int32, (N_KEYS, tm), 0).astype(F32)
    _topk_rounds([(s0_sc, N_KEYS, key_iota, sv0, si0), (s1_sc, N_KEYS, key_iota, sv1, si1)])

    row_iota = lambda n: lax.broadcasted_iota(jnp.int32, (n, tm), 0).astype(F32)
    flat_parts, off = [], 0
    for blk in CAND_BLOCKS:
        if blk[0] == "row":
            _, i, nj = blk
            c_sc[off:off + nj, :] = sv0[i:i + 1, :] + sv1[0:nj, :]
            flat_parts.append(row_iota(nj) + float(i * PEER_TOPK))
            off += nj
        else:
            _, j, i0, imin = blk
            ii = row_iota(SUBLANES) + float(i0)
            c_sc[off:off + SUBLANES, :] = jnp.where(ii >= float(imin), sv0[i0:i0 + SUBLANES, :] + sv1[j:j + 1, :], -jnp.inf)
            flat_parts.append(ii * float(PEER_TOPK) + float(j))
            off += SUBLANES
    _topk_rounds([(c_sc, CAND_ROWS, jnp.concatenate(flat_parts, axis=0), cv, ci)])
    cidx = ci[...]
    i1 = jnp.floor(cidx * (1.0 / PEER_TOPK))
    i2 = cidx - i1 * PEER_TOPK
    e1 = jnp.zeros_like(cidx)
    e2 = jnp.zeros_like(cidx)
    for i in range(PEER_TOPK):
        e1 = jnp.where(i1 == float(i), si0[i:i + 1, :], e1)
        e2 = jnp.where(i2 == float(i), si1[i:i + 1, :], e2)
    idx_ref[0] = (e1 * float(N_KEYS) + e2).astype(jnp.int32)
    cvv = cv[...]
    e = jnp.exp(cvv - cvv[0:1, :])
    g_ref[0] = e / jnp.sum(e, axis=0, keepdims=True)


def _route(x, whi, wlo, khi, klo):
    n, d = x.shape
    tm = ROW_BLOCK
    out_spec = pl.BlockSpec((1, PEER_TOPK, tm), lambda i, h: (h, 0, i))
    key_spec = pl.BlockSpec((1, 2, N_KEYS, PEER_DQ // 2), lambda i, h: (h, 0, 0, 0))
    tk = pltpu.VMEM((PEER_TOPK, tm), F32)
    return pl.pallas_call(
        _route_kernel,
        grid=(n // tm, PEER_HEADS),
        in_specs=[pl.BlockSpec((tm, d), lambda i, h: (i, 0)), _full(whi.shape), _full(wlo.shape), key_spec, key_spec],
        out_specs=[out_spec, out_spec],
        out_shape=[jax.ShapeDtypeStruct((PEER_HEADS, PEER_TOPK, n), jnp.int32),
                   jax.ShapeDtypeStruct((PEER_HEADS, PEER_TOPK, n), F32)],
        scratch_shapes=[pltpu.VMEM((PEER_HEADS, tm, PEER_DQ), F32), pltpu.VMEM((N_KEYS, tm), F32),
                        pltpu.VMEM((N_KEYS, tm), F32), pltpu.VMEM((CAND_ROWS, tm), F32), tk, tk, tk, tk, tk, tk],
        compiler_params=_cparams(("parallel", "arbitrary")),
        name="peer_route",
    )(x, whi, wlo, khi, klo)


ROW_WORDS = 4
SLOTS = PEER_HEADS * PEER_TOPK


def _gather_rows(idx_ref, tab_ref, g_sc, t):
    grp = 8
    for g in range(SLOTS // grp):
        idx_g = idx_ref.at[pl.ds(t * SLOTS + g * grp, grp)]
        for j in range(grp):
            r = g * grp + j
            e4 = pl.multiple_of(idx_g[j], ROW_WORDS)
            g_sc[pl.ds(ROW_WORDS * r, ROW_WORDS), :] = tab_ref[pl.ds(e4, ROW_WORDS), :]


def _token_pipeline(tb, gather, contract, finish):
    gather(0, 0)
    gather(1, 1)

    def body(i, carry):
        t = 2 * i
        r0 = contract(t, 0)
        r1 = contract(t + 1, 1)
        gather(jnp.minimum(t + 2, tb - 1), 0)
        gather(jnp.minimum(t + 3, tb - 1), 1)
        finish(t, r0)
        finish(t + 1, r1)
        return carry

    lax.fori_loop(0, tb // 2, body, 0)


def _peer_u_kernel(idx_ref, x_ref, tab_ref, sel_ref, act_ref, g0_sc, g1_sc, d_sc):
    tb = x_ref.shape[0]
    bufs = (g0_sc, g1_sc)
    rows16 = 2 * SUBLANES
    sub = lax.broadcasted_iota(jnp.int32, (rows16, SUBLANES * SLOTS), 0)
    lane = lax.broadcasted_iota(jnp.int32, (rows16, SUBLANES * SLOTS), 1)
    diag = _imod(lane, SUBLANES) == _imod(sub, SUBLANES)

    def gather(t, b):
        _gather_rows(idx_ref, tab_ref, bufs[b], t)

    def contract(t, b):
        gb = pltpu.bitcast(bufs[b][...], BF16)
        return _dot_nt(x_ref[t], gb)

    def finish(t, d):
        d_sc[pl.ds(t, 1), :] = jnp.sum(jnp.where(diag, d, 0.0), axis=0, keepdims=True)

    _token_pipeline(tb, gather, contract, finish)
    act_ref[...] = _dot_exact_rhs01(d_sc[...], sel_ref[...])


def _peer_v_kernel(idx_ref, act_ref, gate_ref, tab_ref, selt_ref, o_ref, g0_sc, g1_sc, w_sc):
    tb = act_ref.shape[0]
    bufs = (g0_sc, g1_sc)
    rows16 = 2 * SUBLANES
    w = gate_ref[...] * jax.nn.gelu(act_ref[...], approximate=True)
    w_sc[...] = _dot_exact_rhs01(w, selt_ref[...])
    sub = lax.broadcasted_iota(jnp.int32, (rows16, SUBLANES * SLOTS), 0)
    lane = lax.broadcasted_iota(jnp.int32, (rows16, SUBLANES * SLOTS), 1)
    diag = _imod(lane, SUBLANES) == _imod(sub, SUBLANES)
    top = sub < SUBLANES

    def gather(t, b):
        _gather_rows(idx_ref, tab_ref, bufs[b], t)

    def contract(t, b):
        gb = pltpu.bitcast(bufs[b][...], BF16)
        w16 = jnp.where(diag, jnp.broadcast_to(w_sc[pl.ds(t, 1), :], diag.shape), 0.0)
        hi = w16.astype(BF16).astype(F32)
        lhs = jnp.where(top, hi, w16 - hi).astype(BF16)
        return _dot(lhs, gb)

    def finish(t, o16):
        o_ref[pl.ds(pl.multiple_of(t * SUBLANES, SUBLANES), SUBLANES), :] = o16[0:SUBLANES] + o16[SUBLANES:]

    _token_pipeline(tb, gather, contract, finish)


def _table_spec(shape):
    return pl.BlockSpec(shape, lambda i: (0, 0), pipeline_mode=pl.Buffered(1))


def _peer_u(idx4, x16, tab, sel):
    n = x16.shape[0]
    tb = PEER_TB
    return pl.pallas_call(
        _peer_u_kernel,
        grid=(n // tb,),
        in_specs=[pl.BlockSpec((tb * SLOTS,), lambda i: (i,), memory_space=pltpu.SMEM),
                  pl.BlockSpec((tb, 2 * SUBLANES, LANES), lambda i: (i, 0, 0)),
                  _table_spec(tab.shape), _full(sel.shape)],
        out_specs=pl.BlockSpec((tb, SLOTS), lambda i: (i, 0)),
        out_shape=jax.ShapeDtypeStruct((n, SLOTS), F32),
        scratch_shapes=[pltpu.VMEM((ROW_WORDS * SLOTS, LANES), jnp.uint32), pltpu.VMEM((ROW_WORDS * SLOTS, LANES), jnp.uint32),
                        pltpu.VMEM((tb, SUBLANES * SLOTS), F32)],
        compiler_params=_cparams(("parallel",), vmem=56 * 1024 * 1024),
        name="peer_u",
    )(idx4, x16, tab, sel)


def _peer_v(idx4, act, gate, tab, selt):
    n = act.shape[0]
    tb = PEER_TB
    return pl.pallas_call(
        _peer_v_kernel,
        grid=(n // tb,),
        in_specs=[pl.BlockSpec((tb * SLOTS,), lambda i: (i,), memory_space=pltpu.SMEM),
                  pl.BlockSpec((tb, SLOTS), lambda i: (i, 0)), pl.BlockSpec((tb, SLOTS), lambda i: (i, 0)),
                  _table_spec(tab.shape), _full(selt.shape)],
        out_specs=pl.BlockSpec((tb * SUBLANES, LANES), lambda i: (i, 0)),
        out_shape=jax.ShapeDtypeStruct((n * SUBLANES, LANES), F32),
        scratch_shapes=[pltpu.VMEM((ROW_WORDS * SLOTS, LANES), jnp.uint32), pltpu.VMEM((ROW_WORDS * SLOTS, LANES), jnp.uint32),
                        pltpu.VMEM((tb, SUBLANES * SLOTS), F32)],
        compiler_params=_cparams(("parallel",), vmem=56 * 1024 * 1024),
        name="peer_v",
    )(idx4, act, gate, tab, selt)


def _pack_table(t):
    e, d = t.shape
    pair = t.astype(BF16).reshape(e, 2, d // 2).transpose(0, 2, 1)
    return lax.bitcast_convert_type(pair, jnp.uint32).reshape(e * ROW_WORDS, LANES)


def _to_chunk_order(x):
    n, d = x.shape
    return x.reshape(n, 2, ROW_WORDS, LANES).transpose(0, 2, 1, 3).reshape(n, 2 * ROW_WORDS, LANES)


def _from_chunk_order(y, n):
    return y.reshape(n, ROW_WORDS, 2, LANES).transpose(0, 2, 1, 3).reshape(n, 2 * ROW_WORDS * LANES)


def _peer(x, whi, wlo, khi, klo, tab_u, tab_v):
    n = x.shape[0]
    idx_t, gate_t = _route(x, whi, wlo, khi, klo)
    idx4 = (idx_t.reshape(SLOTS, n).T * ROW_WORDS).reshape(n * SLOTS)
    gate = gate_t.reshape(SLOTS, n).T
    xc = _to_chunk_order(x)
    xh = xc.astype(BF16)
    xl = (xc - xh.astype(F32)).astype(BF16)
    x16 = jnp.concatenate([xh, xl], axis=1)
    grp = jnp.arange(SUBLANES * SLOTS) // SUBLANES
    sel = (grp[:, None] == jnp.arange(SLOTS)[None, :]).astype(BF16)
    act = _peer_u(idx4, x16, tab_u, sel)
    y = _peer_v(idx4, act, gate, tab_v, sel.T)
    return _from_chunk_order(y, n)


def _pad_rows(x, n):
    return jnp.pad(x, ((0, n - x.shape[0]),) + ((0, 0),) * (x.ndim - 1))


def _rope_tables(pos):
    half = QK_ROPE // 2
    inv = ROPE_THETA ** (-jnp.arange(half, dtype=F32) / half)
    ang = pos.astype(F32)[:, None] * inv
    reps = LANES // half
    return jnp.tile(jnp.cos(ang), (1, reps)), jnp.tile(jnp.sin(ang), (1, reps))


def _layer_weights(w_in, w_qb, w_kvb):
    d = w_in.shape[0]
    o_ckv = Q_RANK
    o_kr = o_ckv + KV_RANK
    o_mq = o_kr + QK_ROPE
    o_g = o_mq + 4 * ML_W
    small = jnp.concatenate([w_in[:, o_kr:o_mq], w_in[:, o_g:o_g + 2 * ML_HEADS],
                             jnp.zeros((d, LANES - QK_ROPE - 2 * ML_HEADS), w_in.dtype)], axis=1)
    win = jnp.concatenate([w_in[:, :o_kr], small, w_in[:, o_mq:o_g]], axis=1).astype(BF16)
    half = QK_ROPE // 2
    wq = w_qb.reshape(Q_RANK, MLA_HEADS, QK_NOPE + QK_ROPE)
    wqb = jnp.concatenate([wq[..., :QK_NOPE].reshape(Q_RANK, -1), wq[..., QK_NOPE:QK_NOPE + half].reshape(Q_RANK, -1),
                           wq[..., QK_NOPE + half:].reshape(Q_RANK, -1)], axis=1).astype(BF16)
    wkv = w_kvb.reshape(KV_RANK, MLA_HEADS, QK_NOPE + V_DIM)
    eye = jnp.eye(MLA_HEADS, dtype=w_kvb.dtype)
    wuk = jnp.einsum('chd,hg->hdgc', wkv[..., :QK_NOPE], eye).reshape(MLA_HEADS * QK_NOPE, MLA_HEADS * KV_RANK)
    wuv = jnp.einsum('chv,hg->hcgv', wkv[..., QK_NOPE:], eye).reshape(MLA_HEADS * KV_RANK, MLA_W)
    return win, wqb, wuk.astype(BF16), wuv.astype(BF16)


def _hi_lo(w):
    hi = w.astype(BF16)
    return hi, (w - hi.astype(F32)).astype(BF16)


def kernel(x_prompt, x_sample, cache_ckv, cache_kpe, page_table, state_C, state_n, state_m, meta_tokens,
           ln_in_g, ln_in_b, w_in, b_gates, q_norm_g, w_qb, kv_norm_g, w_kvb, mh_norm_g, w_o, ln1_g, ln1_b,
           w_pq, sub_keys, expert_u, expert_v, ln2_g, ln2_b):
    B, seq, D = x_prompt.shape
    DB, S = x_sample.shape[:2]
    depth = w_in.shape[0]
    T = seq + N_META
    page = cache_ckv.shape[2]
    past_len = page_table.shape[1] * page
    alpha = (2.0 * depth) ** 0.25
    n_p, n_s = B * T, DB * S
    n_rows = n_p + n_s
    n_pad = -(-n_rows // ROW_BLOCK) * ROW_BLOCK
    row = lambda a: a.reshape(1, -1)

    meta = jnp.broadcast_to(meta_tokens[None].astype(x_prompt.dtype), (B, N_META, D))
    x = jnp.concatenate([jnp.concatenate([meta, x_prompt], axis=1).reshape(n_p, D), x_sample.reshape(n_s, D)], axis=0)
    x = _pad_rows(x, n_pad)
    pos = jnp.concatenate([jnp.tile(jnp.arange(T), B), jnp.tile(past_len + jnp.arange(S), DB),
                           jnp.zeros((n_pad - n_rows,), jnp.int32)])
    cos_t, sin_t = _rope_tables(pos)

    t_att = -(-T // ATT_TK) * ATT_TK
    nc_p = -(-T // ML_CHUNK)
    t_ml = nc_p * ML_CHUNK
    front = t_ml - T
    s_pad = ML_CHUNK
    new_pad = 16

    outs = {k: [] for k in ("ckv_p", "kpe_p", "C_p", "n_p", "m_p", "ckv_s", "kpe_s", "C_s", "n_s", "m_s")}
    h = x
    for l in range(depth):
        win, wqb, wuk, wuv = _layer_weights(w_in[l], w_qb[l], w_kvb[l])
        (h, ckv, kpe, zs, qlat, qrope, mq, mk, mv, og) = _proj(
            h, cos_t, sin_t, row(ln_in_g), row(ln_in_b), win, row(q_norm_g[l]), wqb, wuk, row(kv_norm_g[l]),
            apply_ln=(l == 0))
        gates = zs[:, QK_ROPE:QK_ROPE + 2 * ML_HEADS]
        bias = b_gates[l].astype(F32).reshape(1, 2 * ML_HEADS, 1)

        half = QK_ROPE // 2
        qcat = jnp.concatenate([qlat.reshape(n_pad, MLA_HEADS, KV_RANK),
                                qrope[:, :LANES].reshape(n_pad, MLA_HEADS, half),
                                qrope[:, LANES:].reshape(n_pad, MLA_HEADS, half)], axis=-1)
        kcat = jnp.concatenate([ckv, kpe], axis=-1).astype(BF16)
        dk = KV_RANK + QK_ROPE

        q_p = jnp.pad(qcat[:n_p].reshape(B, T, MLA_HEADS, dk), ((0, 0), (0, t_att - T), (0, 0), (0, 0)))
        k_p = jnp.pad(kcat[:n_p].reshape(B, T, dk), ((0, 0), (0, t_att - T), (0, 0)))
        kt_p = k_p.reshape(B, t_att // ATT_TK, ATT_TK, dk).transpose(0, 1, 3, 2)
        v_p = jnp.concatenate([k_p[..., :KV_RANK], jnp.ones((B, t_att, KV_RANK), BF16)], axis=-1)
        o_p = _attn_prompt(q_p.transpose(0, 2, 1, 3), kt_p, v_p)
        olat_p = o_p[:, :, :T].transpose(0, 2, 1, 3).reshape(n_p, MLA_HEADS * KV_RANK)

        q_s = qcat[n_p:n_rows].reshape(DB, S, MLA_HEADS, dk).transpose(0, 2, 1, 3).reshape(DB, MLA_HEADS * S, dk)
        k_s = jnp.pad(kcat[n_p:n_rows].reshape(DB, S, dk), ((0, 0), (0, new_pad - S), (0, 0)))
        o_s = _attn_sample(page_table, q_s, k_s, cache_ckv[l], cache_kpe[l], S)
        olat_s = o_s.reshape(DB, MLA_HEADS, S, KV_RANK).transpose(0, 2, 1, 3).reshape(n_s, MLA_HEADS * KV_RANK)
        olat = _pad_rows(jnp.concatenate([olat_p, olat_s], axis=0), n_pad)

        def seqs(a, nb, t, pad):
            return jnp.pad(a.reshape(nb, t, a.shape[-1]), ((0, 0), pad, (0, 0)))

        def gate_forms(g_rows, nb, t, pad, nc):
            gp = seqs(g_rows, nb, t, pad).reshape(nb * nc, ML_CHUNK, 2 * ML_HEADS).transpose(0, 2, 1)
            valid = seqs(jnp.ones((nb * t, 1), F32), nb, t, pad).reshape(nb * nc, 1, ML_CHUNK)
            rf = _gate_prep(gp, bias, valid).reshape(nb, nc, 2 * ML_HEADS, ML_CHUNK)
            return rf, rf.transpose(0, 1, 3, 2)

        rf_p, cf_p = gate_forms(gates[:n_p], B, T, (front, 0), nc_p)
        zero_c = jnp.zeros((B, ML_HEADS, ML_DH, ML_DH), F32)
        h_p, C_p, nn_p, m_p = _mlstm(seqs(mq[:n_p], B, T, (front, 0)), seqs(mk[:n_p], B, T, (front, 0)),
                                     seqs(mv[:n_p], B, T, (front, 0)), rf_p, cf_p,
                                     zero_c, jnp.zeros((B, ML_HEADS, ML_DH), F32), jnp.zeros((B, 1, ML_HEADS), F32))
        rf_s, cf_s = gate_forms(gates[n_p:n_rows], DB, S, (0, s_pad - S), 1)
        h_s, C_s, nn_s, m_s = _mlstm(seqs(mq[n_p:n_rows], DB, S, (0, s_pad - S)), seqs(mk[n_p:n_rows], DB, S, (0, s_pad - S)),
                                     seqs(mv[n_p:n_rows], DB, S, (0, s_pad - S)), rf_s, cf_s,
                                     state_C[l].astype(F32), state_n[l].astype(F32),
                                     state_m[l].astype(F32).reshape(DB, 1, ML_HEADS))
        hml = _pad_rows(jnp.concatenate([h_p[:, front:].reshape(n_p, ML_W), h_s[:, :S].reshape(n_s, ML_W)], axis=0), n_pad)

        h1 = _mix(olat, hml, og, h, wuv, row(mh_norm_g[l]), w_o[l].astype(BF16), row(ln1_g[l]), row(ln1_b[l]), alpha)
        whi, wlo = _hi_lo(w_pq[l])
        khi, klo = _hi_lo(sub_keys[l])
        y = _peer(h1, whi, wlo, khi, klo, _pack_table(expert_u[l]), _pack_table(expert_v[l]))
        h = _res_ln(h1, y, row(ln2_g[l]), row(ln2_b[l]), alpha)

        dt = x_prompt.dtype
        outs["ckv_p"].append(ckv[:n_p].reshape(B, T, KV_RANK))
        outs["kpe_p"].append(kpe[:n_p].reshape(B, T, QK_ROPE))
        outs["C_p"].append(C_p.astype(dt)); outs["n_p"].append(nn_p.astype(dt)); outs["m_p"].append(m_p.reshape(B, ML_HEADS).astype(dt))
        outs["ckv_s"].append(ckv[n_p:n_rows].reshape(DB, S, KV_RANK))
        outs["kpe_s"].append(kpe[n_p:n_rows].reshape(DB, S, QK_ROPE))
        outs["C_s"].append(C_s.astype(dt)); outs["n_s"].append(nn_s.astype(dt)); outs["m_s"].append(m_s.reshape(DB, ML_HEADS).astype(dt))

    y_prompt = h[:n_p].reshape(B, T, D)[:, N_META:]
    y_sample = h[n_p:n_rows].reshape(DB, S, D)
    st = lambda k: jnp.stack(outs[k])
    return (y_prompt, y_sample, st("ckv_p"), st("kpe_p"), st("C_p"), st("n_p"), st("m_p"),
            st("ckv_s"), st("kpe_s"), st("C_s"), st("n_s"), st("m_s"))
```

```python
import functools

import jax
import jax.numpy as jnp
from jax import lax
from jax.experimental import pallas as pl
from jax.experimental.pallas import tpu as pltpu

F32 = jnp.float32
BF16 = jnp.bfloat16

N_META = 16
MLA_HEADS = 8
QK_NOPE = 64
QK_ROPE = 32
V_DIM = 64
Q_RANK = 256
KV_RANK = 128
ROPE_THETA = 10000.0
SM_SCALE = (QK_NOPE + QK_ROPE) ** -0.5
ML_HEADS = 4
ML_DH = 128
ML_W = ML_HEADS * ML_DH
MLA_W = MLA_HEADS * V_DIM
PEER_HEADS = 8
N_KEYS = 128
PEER_DQ = 256
PEER_TOPK = 16
EPS = 1e-5

LANES = 128
SUBLANES = 8
ROW_BLOCK = 256
ATT_TQ = 128
ATT_TK = 256
ML_CHUNK = 128
PAGES_PER_STEP = 16
PEER_TB = 128
PEER_PIPE = 8
NEG_BIG = -1e30
VMEM_LIMIT = 48 * 1024 * 1024

ZW = 2560
Z_CQ, Z_CKV, Z_SMALL, Z_MQ, Z_MK, Z_MV, Z_MO = 0, 256, 384, 512, 1024, 1536, 2048


def _dot(a, b):
    return jnp.dot(a, b, preferred_element_type=F32)


def _dot_nt(a, b):
    return lax.dot_general(a, b, (((1,), (1,)), ((), ())), preferred_element_type=F32)


def _dot_tn(a, b):
    return lax.dot_general(a, b, (((0,), (0,)), ((), ())), preferred_element_type=F32)


def _split3(x):
    a = x.astype(BF16)
    r = x - a.astype(F32)
    b = r.astype(BF16)
    c = (r - b.astype(F32)).astype(BF16)
    return a, b, c


def _dot_exact_rhs01(x, sel):
    a, b, c = _split3(x)
    return _dot(a, sel) + _dot(b, sel) + _dot(c, sel)


def _imod(i, n):
    if n & (n - 1) == 0:
        return jnp.bitwise_and(i, n - 1)
    f = i.astype(F32)
    return (f - n * jnp.floor(f / n)).astype(jnp.int32)


def _layer_norm(x, g, b):
    mu = jnp.mean(x, axis=-1, keepdims=True)
    xc = x - mu
    var = jnp.mean(xc * xc, axis=-1, keepdims=True)
    return xc * lax.rsqrt(var + EPS) * g + b


def _rms_norm(x, g):
    return x * lax.rsqrt(jnp.mean(x * x, axis=-1, keepdims=True) + 1e-6) * g


def _cparams(sem, vmem=VMEM_LIMIT):
    return pltpu.CompilerParams(dimension_semantics=sem, vmem_limit_bytes=vmem)


def _full(shape):
    n = len(shape)
    return pl.BlockSpec(shape, lambda *_: (0,) * n)


def _proj_kernel(xm_ref, xs_ref, cos_ref, sin_ref, lng_ref, lnb_ref, win_ref, qng_ref, wqb_ref, wuk_ref, kvg_ref,
                 h_ref, ckv_ref, kpe_ref, gates_ref, qlat_ref, qrope_ref, mq_ref, mk_ref, mv_ref, og_ref,
                 *, apply_ln, nb_main):
    x = jnp.where(pl.program_id(0) < nb_main, xm_ref[...], xs_ref[...])
    h = _layer_norm(x, lng_ref[...], lnb_ref[...]) if apply_ln else x
    h_ref[...] = h
    z = _dot(h.astype(BF16), win_ref[...])
    ckv_ref[...] = _rms_norm(z[:, Z_CKV:Z_SMALL], kvg_ref[...])
    q = _dot(_rms_norm(z[:, Z_CQ:Z_CKV], qng_ref[...]).astype(BF16), wqb_ref[...])
    nope_w = MLA_HEADS * QK_NOPE
    qlat_ref[...] = _dot(q[:, :nope_w].astype(BF16), wuk_ref[...]).astype(BF16)
    c = cos_ref[...]
    s = sin_ref[...]
    x1 = q[:, nope_w:nope_w + LANES]
    x2 = q[:, nope_w + LANES:nope_w + 2 * LANES]
    qrope_ref[:, 0:LANES] = (x1 * c - x2 * s).astype(BF16)
    qrope_ref[:, LANES:2 * LANES] = (x1 * s + x2 * c).astype(BF16)
    zs = z[:, Z_SMALL:Z_MQ]
    half = QK_ROPE // 2
    lane = lax.broadcasted_iota(jnp.int32, zs.shape, 1)
    first = lane < half
    partner = jnp.where(first, pltpu.roll(zs, LANES - half, 1), pltpu.roll(zs, half, 1))
    kpe = zs * c + partner * jnp.where(first, -s, s)
    kpe_ref[...] = kpe[:, 0:QK_ROPE]
    gates_ref[...] = zs
    mq_ref[...] = z[:, Z_MQ:Z_MK].astype(BF16)
    mk_ref[...] = (z[:, Z_MK:Z_MV] * (ML_DH ** -0.5)).astype(BF16)
    mv_ref[...] = z[:, Z_MV:Z_MO].astype(BF16)
    og_ref[...] = jax.nn.sigmoid(z[:, Z_MO:ZW])


def _proj(x_main, x_small, cos_t, sin_t, lng, lnb, win, qng, wqb, wuk, kvg, *, apply_ln):
    n, d = cos_t.shape[0], x_main.shape[1]
    tm = ROW_BLOCK
    nb_main = n // tm - x_small.shape[0] // tm
    rows = lambda w: pl.BlockSpec((tm, w), lambda i: (i, 0))
    out_w = [(d, F32), (KV_RANK, F32), (QK_ROPE, F32), (LANES, F32), (MLA_HEADS * KV_RANK, BF16),
             (2 * LANES, BF16), (ML_W, BF16), (ML_W, BF16), (ML_W, BF16), (ML_W, F32)]
    return pl.pallas_call(
        functools.partial(_proj_kernel, apply_ln=apply_ln, nb_main=nb_main),
        grid=(n // tm,),
        in_specs=[pl.BlockSpec((tm, d), lambda i: (jnp.minimum(i, nb_main - 1), 0)),
                  pl.BlockSpec((tm, d), lambda i: (jnp.maximum(i - nb_main, 0), 0)), rows(LANES), rows(LANES),
                  _full(lng.shape), _full(lnb.shape), _full(win.shape), _full(qng.shape), _full(wqb.shape),
                  _full(wuk.shape), _full(kvg.shape)],
        out_specs=[rows(w) for w, _ in out_w],
        out_shape=[jax.ShapeDtypeStruct((n, w), dt) for w, dt in out_w],
        compiler_params=_cparams(("parallel",)),
        name="proj",
    )(x_main, x_small, cos_t, sin_t, lng, lnb, win, qng, wqb, wuk, kvg)


def _attn_prompt_kernel(q_ref, kt_ref, v_ref, o_ref, m_sc, acc_sc):
    i = pl.program_id(1)
    nh, tq, dk = q_ref.shape[1:]
    tk = ATT_TK
    reps = tk // LANES
    m_sc[...] = jnp.full(m_sc.shape, -jnp.inf, F32)
    acc_sc[...] = jnp.zeros(acc_sc.shape, F32)
    q_pos = i * tq + lax.broadcasted_iota(jnp.int32, (tq, tk), 0)
    col = lax.broadcasted_iota(jnp.int32, (tq, tk), 1)
    n_full = (i * tq + 1) // tk

    def chunk(c, masked):
        kt = kt_ref[0, c]
        v = v_ref[0, pl.ds(pl.multiple_of(c * tk, tk), tk), :]
        for hd in range(nh):
            rows = slice(hd * tq, (hd + 1) * tq)
            s = _dot(q_ref[0, hd], kt) * SM_SCALE
            if masked:
                s = jnp.where(col + c * tk <= q_pos, s, -jnp.inf)
            m_prev = m_sc[rows, :]
            m_new = jnp.maximum(m_prev, jnp.max(s, axis=1, keepdims=True))
            alpha = jnp.exp(m_prev - m_new)
            p = jnp.exp(s - jnp.tile(m_new, (1, reps)))
            acc_sc[rows, :] = jnp.tile(alpha, (1, 2)) * acc_sc[rows, :] + _dot(p.astype(BF16), v)
            m_sc[rows, :] = m_new

    def body(c, carry):
        chunk(c, False)
        return carry

    lax.fori_loop(0, n_full, body, 0)
    chunk(n_full, True)
    for hd in range(nh):
        acc = acc_sc[hd * tq:(hd + 1) * tq, :]
        o_ref[0, hd] = (acc[:, :KV_RANK] / acc[:, KV_RANK:]).astype(o_ref.dtype)


def _attn_prompt(q, kt, v):
    b, nh, tp, dk = q.shape
    tq = ATT_TQ
    return pl.pallas_call(
        _attn_prompt_kernel,
        grid=(b, tp // tq),
        in_specs=[pl.BlockSpec((1, nh, tq, dk), lambda bi, i: (bi, 0, i, 0)),
                  pl.BlockSpec((1,) + kt.shape[1:], lambda bi, i: (bi, 0, 0, 0)),
                  pl.BlockSpec((1,) + v.shape[1:], lambda bi, i: (bi, 0, 0))],
        out_specs=pl.BlockSpec((1, nh, tq, KV_RANK), lambda bi, i: (bi, 0, i, 0)),
        out_shape=jax.ShapeDtypeStruct((b, nh, tp, KV_RANK), BF16),
        scratch_shapes=[pltpu.VMEM((nh * tq, LANES), F32), pltpu.VMEM((nh * tq, 2 * KV_RANK), F32)],
        compiler_params=_cparams(("parallel", "arbitrary")),
        name="attn_prompt",
    )(q, kt, v)


def _attn_sample_kernel(pt_ref, q_ref, knew_ref, *refs, n_new):
    pps = PAGES_PER_STEP
    ckv_refs, kpe_refs = refs[:pps], refs[pps:2 * pps]
    o_ref, m_sc, l_sc, acc_sc = refs[2 * pps:]
    g = pl.program_id(1)
    q = q_ref[0]
    ql, qr = q[:, :KV_RANK], q[:, KV_RANK:]

    @pl.when(g == 0)
    def _():
        m_sc[...] = jnp.full(m_sc.shape, -jnp.inf, F32)
        l_sc[...] = jnp.zeros(l_sc.shape, F32)
        acc_sc[...] = jnp.zeros(acc_sc.shape, F32)

    def update(s, vals):
        m_prev = m_sc[...]
        m_new = jnp.maximum(m_prev, jnp.max(s, axis=1, keepdims=True))
        alpha = jnp.exp(m_prev - m_new)
        p = jnp.exp(s - m_new)
        l_sc[...] = alpha * l_sc[...] + jnp.sum(p, axis=1, keepdims=True)
        acc = alpha * acc_sc[...]
        w = p.shape[1] // len(vals)
        for j, v in enumerate(vals):
            acc = acc + _dot(p[:, j * w:(j + 1) * w].astype(BF16), v)
        acc_sc[...] = acc
        m_sc[...] = m_new

    kc = [r[...].astype(BF16) for r in ckv_refs]
    s = jnp.concatenate([_dot_nt(ql, kc[j]) + _dot(qr, kpe_refs[j][...].astype(BF16)) for j in range(pps)],
                        axis=1) * SM_SCALE
    update(s, kc)

    @pl.when(g == pl.num_programs(1) - 1)
    def _():
        kn = knew_ref[0]
        s2 = _dot_nt(q, kn) * SM_SCALE
        row = lax.broadcasted_iota(jnp.int32, s2.shape, 0)
        colk = lax.broadcasted_iota(jnp.int32, s2.shape, 1)
        s2 = jnp.where((colk <= _imod(row, n_new)) & (colk < n_new), s2, -jnp.inf)
        update(s2, [kn[:, :KV_RANK]])
        o_ref[0] = (acc_sc[...] / l_sc[...]).astype(o_ref.dtype)


def _attn_sample(page_table, q, knew, cache_ckv, cache_kpe_t, n_new):
    db, r, dk = q.shape
    n_pages = page_table.shape[1]
    pps = PAGES_PER_STEP
    newp = knew.shape[1]

    def page_spec(arr, j):
        return pl.BlockSpec((None,) + arr.shape[1:], lambda b, g, pt: (pt[b, g * pps + j], 0, 0))

    grid_spec = pltpu.PrefetchScalarGridSpec(
        num_scalar_prefetch=1,
        grid=(db, n_pages // pps),
        in_specs=[pl.BlockSpec((1, r, dk), lambda b, g, pt: (b, 0, 0)),
                  pl.BlockSpec((1, newp, dk), lambda b, g, pt: (b, 0, 0))]
                 + [page_spec(cache_ckv, j) for j in range(pps)]
                 + [page_spec(cache_kpe_t, j) for j in range(pps)],
        out_specs=pl.BlockSpec((1, r, KV_RANK), lambda b, g, pt: (b, 0, 0)),
        scratch_shapes=[pltpu.VMEM((r, 1), F32), pltpu.VMEM((r, 1), F32), pltpu.VMEM((r, KV_RANK), F32)],
    )
    return pl.pallas_call(
        functools.partial(_attn_sample_kernel, n_new=n_new),
        grid_spec=grid_spec,
        out_shape=jax.ShapeDtypeStruct((db, r, KV_RANK), BF16),
        compiler_params=_cparams(("parallel", "arbitrary")),
        name="attn_sample",
    )(page_table, q, knew, *([cache_ckv] * pps), *([cache_kpe_t] * pps))


def _gate_prep_kernel(g_ref, bias_ref, valid_ref, triu_ref, o_ref):
    x = g_ref[...] + bias_ref[...]
    valid = valid_ref[...] > 0.5
    is_i = lax.broadcasted_iota(jnp.int32, x.shape, 1) < ML_HEADS
    lf = jnp.minimum(x, 0.0) - jnp.log1p(jnp.exp(-jnp.abs(x)))
    y = jnp.where(is_i, jnp.where(valid, x, NEG_BIG), jnp.where(valid, lf, 0.0))
    r, g8, l = x.shape
    cum = _dot_exact_rhs01(y.reshape(r * g8, l), triu_ref[...]).reshape(r, g8, l)
    o_ref[...] = jnp.where(is_i, y, cum)


def _gate_prep(gt, bias, valid):
    r, g8, l = gt.shape
    rb = 8 if r % 8 == 0 else r
    triu = (jnp.arange(l)[:, None] <= jnp.arange(l)[None, :]).astype(BF16)
    return pl.pallas_call(
        _gate_prep_kernel,
        grid=(r // rb,),
        in_specs=[pl.BlockSpec((rb, g8, l), lambda i: (i, 0, 0)), _full(bias.shape),
                  pl.BlockSpec((rb, 1, l), lambda i: (i, 0, 0)), _full(triu.shape)],
        out_specs=pl.BlockSpec((rb, g8, l), lambda i: (i, 0, 0)),
        out_shape=jax.ShapeDtypeStruct((r, g8, l), F32),
        compiler_params=_cparams(("parallel",)),
        name="gate_prep",
    )(gt, bias, valid, triu)


def _mlstm_kernel(q_ref, k_ref, v_ref, rf_ref, cf_ref, c0_ref, n0_ref, m0_ref,
                  h_ref, c_ref, n_ref, m_ref, c_sc, n_sc, m_sc):
    n_chunks = rf_ref.shape[1]
    lc = ML_CHUNK
    c_sc[...] = c0_ref[0]
    n_sc[...] = n0_ref[0]
    m_sc[...] = m0_ref[0]
    row = lax.broadcasted_iota(jnp.int32, (lc, lc), 0)
    colm = lax.broadcasted_iota(jnp.int32, (lc, lc), 1)
    causal = row >= colm

    def body(ci, carry):
        start = pl.multiple_of(ci * lc, lc)
        rf = rf_ref[0, ci]
        cf = cf_ref[0, ci]
        for hd in range(ML_HEADS):
            sl = slice(hd * ML_DH, (hd + 1) * ML_DH)
            q = q_ref[0, pl.ds(start, lc), sl]
            k = k_ref[0, pl.ds(start, lc), sl]
            v = v_ref[0, pl.ds(start, lc), sl]
            ig_row, b_row = rf[hd:hd + 1, :], rf[ML_HEADS + hd:ML_HEADS + hd + 1, :]
            ig_col, b_col = cf[:, hd:hd + 1], cf[:, ML_HEADS + hd:ML_HEADS + hd + 1]
            c0 = c_sc[hd]
            n0 = n_sc[hd:hd + 1, :]
            m0 = m_sc[0:1, hd:hd + 1]
            d = jnp.where(causal, b_col - b_row + ig_row, -jnp.inf)
            inter = b_col + m0
            m = jnp.maximum(inter, jnp.max(d, axis=1, keepdims=True))
            w = jnp.exp(d - m)
            a = jnp.exp(inter - m)
            wqk = w * _dot_nt(q, k)
            num = a * _dot_nt(q, c0.astype(BF16)) + _dot(wqk.astype(BF16), v)
            qf = q.astype(F32)
            den = a * jnp.sum(qf * n0, axis=1, keepdims=True) + jnp.sum(wqk, axis=1, keepdims=True)
            h_ref[0, pl.ds(start, lc), sl] = num / jnp.maximum(jnp.abs(den), jnp.exp(-m))
            b_last = b_col[lc - 1:lc, :]
            dec_row = b_last - b_row + ig_row
            m_last = jnp.maximum(b_last + m0, jnp.max(dec_row, axis=1, keepdims=True))
            ws_col = jnp.exp(b_last - b_col + ig_col - m_last)
            a_last = jnp.exp(b_last + m0 - m_last)
            c_sc[hd] = a_last * c0 + _dot_tn((v.astype(F32) * ws_col).astype(BF16), k)
            n_sc[hd:hd + 1, :] = a_last * n0 + jnp.sum(ws_col * k.astype(F32), axis=0, keepdims=True)
            m_sc[0:1, hd:hd + 1] = m_last
        return carry

    lax.fori_loop(0, n_chunks, body, 0)
    c_ref[0] = c_sc[...]
    n_ref[0] = n_sc[...]
    m_ref[0] = m_sc[...]


def _mlstm(q, k, v, rf, cf, c0, n0, m0):
    b, t, w = q.shape
    nc = rf.shape[1]
    seq = pl.BlockSpec((1, t, w), lambda i: (i, 0, 0))
    st = [pl.BlockSpec((1,) + c0.shape[1:], lambda i: (i, 0, 0, 0)),
          pl.BlockSpec((1,) + n0.shape[1:], lambda i: (i, 0, 0)),
          pl.BlockSpec((1,) + m0.shape[1:], lambda i: (i, 0, 0))]
    return pl.pallas_call(
        _mlstm_kernel,
        grid=(b,),
        in_specs=[seq, seq, seq,
                  pl.BlockSpec((1,) + rf.shape[1:], lambda i: (i, 0, 0, 0)),
                  pl.BlockSpec((1,) + cf.shape[1:], lambda i: (i, 0, 0, 0))] + st,
        out_specs=[seq] + st,
        out_shape=[jax.ShapeDtypeStruct((b, t, w), F32), jax.ShapeDtypeStruct(c0.shape, F32),
                   jax.ShapeDtypeStruct(n0.shape, F32), jax.ShapeDtypeStruct(m0.shape, F32)],
        scratch_shapes=[pltpu.VMEM(c0.shape[1:], F32), pltpu.VMEM(n0.shape[1:], F32), pltpu.VMEM(m0.shape[1:], F32)],
        compiler_params=_cparams(("parallel",)),
        name="mlstm",
    )(q, k, v, rf, cf, c0, n0, m0)


def _mix_kernel(olat_ref, hml_ref, og_ref, res_ref, wuv_ref, mhg_ref, wo_ref, g_ref, b_ref, o_ref, *, alpha):
    omla = _dot(olat_ref[...], wuv_ref[...])
    hml = hml_ref[...]
    parts = []
    for hd in range(ML_HEADS):
        x = hml[:, hd * ML_DH:(hd + 1) * ML_DH]
        mu = jnp.mean(x, axis=-1, keepdims=True)
        xc = x - mu
        var = jnp.mean(xc * xc, axis=-1, keepdims=True)
        parts.append(xc * lax.rsqrt(var + EPS))
    oml = (jnp.concatenate(parts, axis=1) * mhg_ref[...]) * og_ref[...]
    cat = jnp.concatenate([omla.astype(BF16), oml.astype(BF16)], axis=1)
    a = _dot(cat, wo_ref[...])
    o_ref[...] = _layer_norm(alpha * res_ref[...] + a, g_ref[...], b_ref[...])


def _mix(olat, hml, og, res, wuv, mhg, wo, g, b, alpha):
    n, d = res.shape
    tm = ROW_BLOCK
    rows = lambda w: pl.BlockSpec((tm, w), lambda i: (i, 0))
    return pl.pallas_call(
        functools.partial(_mix_kernel, alpha=alpha),
        grid=(n // tm,),
        in_specs=[rows(olat.shape[1]), rows(ML_W), rows(ML_W), rows(d), _full(wuv.shape), _full(mhg.shape),
                  _full(wo.shape), _full(g.shape), _full(b.shape)],
        out_specs=rows(d),
        out_shape=jax.ShapeDtypeStruct((n, d), F32),
        compiler_params=_cparams(("parallel",)),
        name="mix",
    )(olat, hml, og, res, wuv, mhg, wo, g, b)


def _res_ln_kernel(res_ref, y_ref, g_ref, b_ref, o_ref, *, alpha):
    o_ref[...] = _layer_norm(alpha * res_ref[...] + y_ref[...], g_ref[...], b_ref[...])


def _res_ln(res, y, g, b, alpha):
    n, d = res.shape
    tm = ROW_BLOCK
    rows = pl.BlockSpec((tm, d), lambda i: (i, 0))
    return pl.pallas_call(
        functools.partial(_res_ln_kernel, alpha=alpha),
        grid=(n // tm,),
        in_specs=[rows, rows, _full(g.shape), _full(b.shape)],
        out_specs=rows,
        out_shape=jax.ShapeDtypeStruct((n, d), F32),
        compiler_params=_cparams(("parallel",)),
        name="res_ln",
    )(res, y, g, b)


def _topk_rounds(jobs):
    def body(kk, carry):
        for s_sc, n_rows, tie, vals_sc, idx_sc in jobs:
            s = s_sc[0:n_rows, :]
            m = jnp.max(s, axis=0, keepdims=True)
            i = jnp.min(jnp.where(s == m, tie, jnp.inf), axis=0, keepdims=True)
            vals_sc[pl.ds(kk, 1), :] = m
            idx_sc[pl.ds(kk, 1), :] = i
            s_sc[0:n_rows, :] = jnp.where(tie == i, -jnp.inf, s)
        return carry

    lax.fori_loop(0, PEER_TOPK, body, 0)


def _candidate_blocks():
    k, sub = PEER_TOPK, SUBLANES
    need = [k // (i + 1) for i in range(k)]
    blocks, covered = [], set()
    for i in range(k):
        if need[i] >= sub:
            nj = -(-need[i] // sub) * sub
            blocks.append(("row", i, nj))
            covered |= {(i, j) for j in range(nj)}
    for i0 in range(0, k, sub):
        rest = [i for i in range(i0, i0 + sub) if need[i] < sub]
        if rest:
            for j in range(max(need[i] for i in rest)):
                blocks.append(("col", j, i0, rest[0]))
                covered |= {(i, j) for i in range(rest[0], i0 + sub)}
    assert all((i, j) in covered for i in range(k) for j in range(k) if (i + 1) * (j + 1) <= k)
    return blocks


CAND_BLOCKS = _candidate_blocks()
CAND_ROWS = sum(b[2] if b[0] == "row" else SUBLANES for b in CAND_BLOCKS)


def _route_kernel(x_ref, whi_ref, wlo_ref, khi_ref, klo_ref, idx_ref, g_ref,
                  q_sc, s0_sc, s1_sc, c_sc, sv0, si0, sv1, si1, cv, ci):
    hd = pl.program_id(1)
    kq = PEER_DQ // 2
    tm = x_ref.shape[0]

    @pl.when(hd == 0)
    def _():
        x = x_ref[...]
        xh = x.astype(BF16)
        xl = (x - xh.astype(F32)).astype(BF16)
        q = _dot(xh, whi_ref[...]) + _dot(xh, wlo_ref[...]) + _dot(xl, whi_ref[...])
        for j in range(PEER_HEADS):
            q_sc[j] = q[:, j * PEER_DQ:(j + 1) * PEER_DQ]

    qh_all = q_sc[hd]
    for p, s_sc in enumerate((s0_sc, s1_sc)):
        qq = qh_all[:, p * kq:(p + 1) * kq]
        qh = qq.astype(BF16)
        ql = (qq - qh.astype(F32)).astype(BF16)
        kh, kl = khi_ref[0, p], klo_ref[0, p]
        s_sc[...] = _dot_nt(kh, qh) + _dot_nt(kh, ql) + _dot_nt(kl, qh)
    key_iota = lax.broadcasted_iota(jnp.int32, (N_KEYS, tm), 0).astype(F32)
    _topk_rounds([(s0_sc, N_KEYS, key_iota, sv0, si0), (s1_sc, N_KEYS, key_iota, sv1, si1)])

    row_iota = lambda n: lax.broadcasted_iota(jnp.int32, (n, tm), 0).astype(F32)
    flat_parts, off = [], 0
    for blk in CAND_BLOCKS:
        if blk[0] == "row":
            _, i, nj = blk
            c_sc[off:off + nj, :] = sv0[i:i + 1, :] + sv1[0:nj, :]
            flat_parts.append(row_iota(nj) + float(i * PEER_TOPK))
            off += nj
        else:
            _, j, i0, imin = blk
            ii = row_iota(SUBLANES) + float(i0)
            c_sc[off:off + SUBLANES, :] = jnp.where(ii >= float(imin), sv0[i0:i0 + SUBLANES, :] + sv1[j:j + 1, :], -jnp.inf)
            flat_parts.append(ii * float(PEER_TOPK) + float(j))
            off += SUBLANES
    _topk_rounds([(c_sc, CAND_ROWS, jnp.concatenate(flat_parts, axis=0), cv, ci)])
    cidx = ci[...]
    i1 = jnp.floor(cidx * (1.0 / PEER_TOPK))
    i2 = cidx - i1 * PEER_TOPK
    e1 = jnp.zeros_like(cidx)
    e2 = jnp.zeros_like(cidx)
    for i in range(PEER_TOPK):
        e1 = jnp.where(i1 == float(i), si0[i:i + 1, :], e1)
        e2 = jnp.where(i2 == float(i), si1[i:i + 1, :], e2)
    idx_ref[0] = (e1 * float(N_KEYS) + e2).astype(jnp.int32)
    cvv = cv[...]
    e = jnp.exp(cvv - cvv[0:1, :])
    g_ref[0] = e / jnp.sum(e, axis=0, keepdims=True)


def _route(x, whi, wlo, khi, klo):
    n, d = x.shape
    tm = ROW_BLOCK
    out_spec = pl.BlockSpec((1, PEER_TOPK, tm), lambda i, h: (h, 0, i))
    key_spec = pl.BlockSpec((1, 2, N_KEYS, PEER_DQ // 2), lambda i, h: (h, 0, 0, 0))
    tk = pltpu.VMEM((PEER_TOPK, tm), F32)
    return pl.pallas_call(
        _route_kernel,
        grid=(n // tm, PEER_HEADS),
        in_specs=[pl.BlockSpec((tm, d), lambda i, h: (i, 0)), _full(whi.shape), _full(wlo.shape), key_spec, key_spec],
        out_specs=[out_spec, out_spec],
        out_shape=[jax.ShapeDtypeStruct((PEER_HEADS, PEER_TOPK, n), jnp.int32),
                   jax.ShapeDtypeStruct((PEER_HEADS, PEER_TOPK, n), F32)],
        scratch_shapes=[pltpu.VMEM((PEER_HEADS, tm, PEER_DQ), F32), pltpu.VMEM((N_KEYS, tm), F32),
                        pltpu.VMEM((N_KEYS, tm), F32), pltpu.VMEM((CAND_ROWS, tm), F32), tk, tk, tk, tk, tk, tk],
        compiler_params=_cparams(("parallel", "arbitrary")),
        name="peer_route",
    )(x, whi, wlo, khi, klo)


ROW_WORDS = 4
SLOTS = PEER_HEADS * PEER_TOPK


def _gather_rows(idx_ref, tab_ref, g_sc, t):
    grp = 8
    for g in range(SLOTS // grp):
        idx_g = idx_ref.at[pl.ds(t * SLOTS + g * grp, grp)]
        for j in range(grp):
            r = g * grp + j
            e4 = pl.multiple_of(idx_g[j], ROW_WORDS)
            g_sc[pl.ds(ROW_WORDS * r, ROW_WORDS), :] = tab_ref[pl.ds(e4, ROW_WORDS), :]


def _token_pipeline(tb, depth, gather, contract, finish):
    for b in range(depth):
        gather(b, b)

    def body(i, carry):
        t = depth * i
        res = [contract(t + b, b) for b in range(depth)]
        for b in range(depth):
            gather(jnp.minimum(t + depth + b, tb - 1), b)
        for b in range(depth):
            finish(t + b, res[b])
        return carry

    lax.fori_loop(0, tb // depth, body, 0)


def _diag_mask():
    rows16 = 2 * SUBLANES
    sub = lax.broadcasted_iota(jnp.int32, (rows16, SUBLANES * SLOTS), 0)
    lane = lax.broadcasted_iota(jnp.int32, (rows16, SUBLANES * SLOTS), 1)
    return _imod(lane, SUBLANES) == _imod(sub, SUBLANES), sub < SUBLANES


def _hi_lo_rows(x8):
    hi = x8.astype(BF16).astype(F32)
    return jnp.concatenate([hi, x8 - hi], axis=0).astype(BF16)


def _peer_u_kernel(idx_ref, x_ref, tab_ref, sel_ref, act_ref, *scratch):
    bufs, d_sc = scratch[:-1], scratch[-1]
    tb = x_ref.shape[0]
    diag, _ = _diag_mask()

    def gather(t, b):
        _gather_rows(idx_ref, tab_ref, bufs[b], t)

    def contract(t, b):
        gb = pltpu.bitcast(bufs[b][...], BF16)
        return _dot_nt(_hi_lo_rows(x_ref[t]), gb)

    def finish(t, d):
        d_sc[pl.ds(t, 1), :] = jnp.sum(jnp.where(diag, d, 0.0), axis=0, keepdims=True)

    _token_pipeline(tb, len(bufs), gather, contract, finish)
    act_ref[...] = _dot_exact_rhs01(d_sc[...], sel_ref[...])


def _peer_v_kernel(idx_ref, act_ref, gate_ref, tab_ref, selt_ref, o_ref, *scratch):
    bufs, w_sc = scratch[:-1], scratch[-1]
    tb = act_ref.shape[0]
    w = gate_ref[...] * jax.nn.gelu(act_ref[...], approximate=True)
    w_sc[...] = _dot_exact_rhs01(w, selt_ref[...])
    diag, top = _diag_mask()

    def gather(t, b):
        _gather_rows(idx_ref, tab_ref, bufs[b], t)

    def contract(t, b):
        gb = pltpu.bitcast(bufs[b][...], BF16)
        w16 = jnp.where(diag, jnp.broadcast_to(w_sc[pl.ds(t, 1), :], diag.shape), 0.0)
        hi = w16.astype(BF16).astype(F32)
        lhs = jnp.where(top, hi, w16 - hi).astype(BF16)
        return _dot(lhs, gb)

    def finish(t, o16):
        o_ref[pl.ds(pl.multiple_of(t * SUBLANES, SUBLANES), SUBLANES), :] = o16[0:SUBLANES] + o16[SUBLANES:]

    _token_pipeline(tb, len(bufs), gather, contract, finish)


def _table_spec(shape):
    return pl.BlockSpec(shape, lambda i: (0, 0), pipeline_mode=pl.Buffered(1))


def _peer_scratch(tb):
    gbuf = pltpu.VMEM((ROW_WORDS * SLOTS, LANES), jnp.uint32)
    return [gbuf] * PEER_PIPE + [pltpu.VMEM((tb, SUBLANES * SLOTS), F32)]


def _peer_u(idx4, x8, tab, sel):
    n = x8.shape[0]
    tb = PEER_TB
    return pl.pallas_call(
        _peer_u_kernel,
        grid=(n // tb,),
        in_specs=[pl.BlockSpec((tb * SLOTS,), lambda i: (i,), memory_space=pltpu.SMEM),
                  pl.BlockSpec((tb, SUBLANES, LANES), lambda i: (i, 0, 0)),
                  _table_spec(tab.shape), _full(sel.shape)],
        out_specs=pl.BlockSpec((tb, SLOTS), lambda i: (i, 0)),
        out_shape=jax.ShapeDtypeStruct((n, SLOTS), F32),
        scratch_shapes=_peer_scratch(tb),
        compiler_params=_cparams(("parallel",), vmem=56 * 1024 * 1024),
        name="peer_u",
    )(idx4, x8, tab, sel)


def _peer_v(idx4, act, gate, tab, selt):
    n = act.shape[0]
    tb = PEER_TB
    return pl.pallas_call(
        _peer_v_kernel,
        grid=(n // tb,),
        in_specs=[pl.BlockSpec((tb * SLOTS,), lambda i: (i,), memory_space=pltpu.SMEM),
                  pl.BlockSpec((tb, SLOTS), lambda i: (i, 0)), pl.BlockSpec((tb, SLOTS), lambda i: (i, 0)),
                  _table_spec(tab.shape), _full(selt.shape)],
        out_specs=pl.BlockSpec((tb * SUBLANES, LANES), lambda i: (i, 0)),
        out_shape=jax.ShapeDtypeStruct((n * SUBLANES, LANES), F32),
        scratch_shapes=_peer_scratch(tb),
        compiler_params=_cparams(("parallel",), vmem=56 * 1024 * 1024),
        name="peer_v",
    )(idx4, act, gate, tab, selt)


def _pack_table(t):
    e, d = t.shape
    pair = t.astype(BF16).reshape(e, ROW_WORDS, 2, LANES).transpose(0, 1, 3, 2)
    return lax.bitcast_convert_type(pair, jnp.uint32).reshape(e * ROW_WORDS, LANES)


def _peer(x, whi, wlo, khi, klo, tab_u, tab_v):
    n, d = x.shape
    idx_t, gate_t = _route(x, whi, wlo, khi, klo)
    idx4 = (idx_t.reshape(SLOTS, n).T * ROW_WORDS).reshape(n * SLOTS)
    gate = gate_t.reshape(SLOTS, n).T
    grp = jnp.arange(SUBLANES * SLOTS) // SUBLANES
    sel = (grp[:, None] == jnp.arange(SLOTS)[None, :]).astype(BF16)
    act = _peer_u(idx4, x.reshape(n, SUBLANES, LANES), tab_u, sel)
    y = _peer_v(idx4, act, gate, tab_v, sel.T)
    return y.reshape(n, d)


def _pad_rows(x, n):
    return jnp.pad(x, ((0, n - x.shape[0]),) + ((0, 0),) * (x.ndim - 1))


def _rope_tables(pos):
    half = QK_ROPE // 2
    inv = ROPE_THETA ** (-jnp.arange(half, dtype=F32) / half)
    ang = pos.astype(F32)[:, None] * inv
    reps = LANES // half
    return jnp.tile(jnp.cos(ang), (1, reps)), jnp.tile(jnp.sin(ang), (1, reps))


def _layer_weights(w_in, w_qb, w_kvb):
    d = w_in.shape[0]
    o_ckv = Q_RANK
    o_kr = o_ckv + KV_RANK
    o_mq = o_kr + QK_ROPE
    o_g = o_mq + 4 * ML_W
    small = jnp.concatenate([w_in[:, o_kr:o_mq], w_in[:, o_g:o_g + 2 * ML_HEADS],
                             jnp.zeros((d, LANES - QK_ROPE - 2 * ML_HEADS), w_in.dtype)], axis=1)
    win = jnp.concatenate([w_in[:, :o_kr], small, w_in[:, o_mq:o_g]], axis=1).astype(BF16)
    half = QK_ROPE // 2
    wq = w_qb.reshape(Q_RANK, MLA_HEADS, QK_NOPE + QK_ROPE)
    wqb = jnp.concatenate([wq[..., :QK_NOPE].reshape(Q_RANK, -1), wq[..., QK_NOPE:QK_NOPE + half].reshape(Q_RANK, -1),
                           wq[..., QK_NOPE + half:].reshape(Q_RANK, -1)], axis=1).astype(BF16)
    wkv = w_kvb.reshape(KV_RANK, MLA_HEADS, QK_NOPE + V_DIM)
    eye = jnp.eye(MLA_HEADS, dtype=w_kvb.dtype)
    wuk = jnp.einsum('chd,hg->hdgc', wkv[..., :QK_NOPE], eye).reshape(MLA_HEADS * QK_NOPE, MLA_HEADS * KV_RANK)
    wuv = jnp.einsum('chv,hg->hcgv', wkv[..., QK_NOPE:], eye).reshape(MLA_HEADS * KV_RANK, MLA_W)
    return win, wqb, wuk.astype(BF16), wuv.astype(BF16)


def _hi_lo(w):
    hi = w.astype(BF16)
    return hi, (w - hi.astype(F32)).astype(BF16)


def kernel(x_prompt, x_sample, cache_ckv, cache_kpe, page_table, state_C, state_n, state_m, meta_tokens,
           ln_in_g, ln_in_b, w_in, b_gates, q_norm_g, w_qb, kv_norm_g, w_kvb, mh_norm_g, w_o, ln1_g, ln1_b,
           w_pq, sub_keys, expert_u, expert_v, ln2_g, ln2_b):
    B, seq, D = x_prompt.shape
    DB, S = x_sample.shape[:2]
    depth = w_in.shape[0]
    dt = x_prompt.dtype
    T = seq + N_META
    page = cache_ckv.shape[2]
    past_len = page_table.shape[1] * page
    alpha = (2.0 * depth) ** 0.25
    row = lambda a: a.reshape(1, -1)

    n_pt, n_s = B * seq, DB * S
    assert n_pt % ROW_BLOCK == 0, "prompt rows must fill whole row blocks"
    m0 = n_pt + n_s
    n_small = -(-(n_s + N_META) // ROW_BLOCK) * ROW_BLOCK
    n_pad = n_pt + n_small
    x_small = _pad_rows(jnp.concatenate([x_sample.reshape(n_s, D), meta_tokens.astype(dt)], axis=0), n_small)
    pos = jnp.concatenate([jnp.tile(N_META + jnp.arange(seq), B), jnp.tile(past_len + jnp.arange(S), DB),
                           jnp.arange(N_META), jnp.zeros((n_pad - m0 - N_META,), jnp.int32)])
    cos_t, sin_t = _rope_tables(pos)

    t_att = -(-T // ATT_TK) * ATT_TK
    nc_p = -(-T // ML_CHUNK)
    t_ml = nc_p * ML_CHUNK
    new_pad = 16

    def prompt_seqs(a, t_total):
        w = a.shape[-1]
        parts = [jnp.broadcast_to(a[m0:m0 + N_META][None], (B, N_META, w)), a[:n_pt].reshape(B, seq, w)]
        if t_total > T:
            parts.append(jnp.zeros((B, t_total - T, w), a.dtype))
        return jnp.concatenate(parts, axis=1)

    def sample_seqs(a, t_total):
        return jnp.pad(a[n_pt:m0].reshape(DB, S, a.shape[-1]), ((0, 0), (0, t_total - S), (0, 0)))

    def flat_rows(prompt_part, sample_part, meta_part):
        w = meta_part.shape[-1]
        return _pad_rows(jnp.concatenate([prompt_part.reshape(n_pt, w), sample_part.reshape(n_s, w), meta_part], axis=0),
                         n_pad)

    def gate_forms(g_seq, n_valid):
        nb, t_total = g_seq.shape[:2]
        nc = t_total // ML_CHUNK
        gp = g_seq.reshape(nb * nc, ML_CHUNK, 2 * ML_HEADS).transpose(0, 2, 1)
        valid = jnp.broadcast_to((jnp.arange(t_total) < n_valid).astype(F32)[None], (nb, t_total)).reshape(nb * nc, 1, ML_CHUNK)
        rf = _gate_prep(gp, bias, valid).reshape(nb, nc, 2 * ML_HEADS, ML_CHUNK)
        return rf, rf.transpose(0, 1, 3, 2)

    outs = {k: [] for k in ("ckv_p", "kpe_p", "C_p", "n_p", "m_p", "ckv_s", "kpe_s", "C_s", "n_s", "m_s")}
    h_main, h_small = x_prompt.reshape(n_pt, D), x_small
    for l in range(depth):
        win, wqb, wuk, wuv = _layer_weights(w_in[l], w_qb[l], w_kvb[l])
        (h, ckv, kpe, zs, qlat, qrope, mq, mk, mv, og) = _proj(
            h_main, h_small, cos_t, sin_t, row(ln_in_g), row(ln_in_b), win, row(q_norm_g[l]), wqb, wuk,
            row(kv_norm_g[l]), apply_ln=(l == 0))
        gates = zs[:, QK_ROPE:QK_ROPE + 2 * ML_HEADS]
        bias = b_gates[l].astype(F32).reshape(1, 2 * ML_HEADS, 1)

        half = QK_ROPE // 2
        dk = KV_RANK + QK_ROPE
        qcat = jnp.concatenate([qlat.reshape(n_pad, MLA_HEADS, KV_RANK),
                                qrope[:, :LANES].reshape(n_pad, MLA_HEADS, half),
                                qrope[:, LANES:].reshape(n_pad, MLA_HEADS, half)], axis=-1)
        kcat = jnp.concatenate([ckv, kpe], axis=-1).astype(BF16)

        q_p = prompt_seqs(qcat.reshape(n_pad, MLA_HEADS * dk), t_att).reshape(B, t_att, MLA_HEADS, dk)
        k_p = prompt_seqs(kcat, t_att)
        kt_p = k_p.reshape(B, t_att // ATT_TK, ATT_TK, dk).transpose(0, 1, 3, 2)
        v_p = jnp.concatenate([k_p[..., :KV_RANK], jnp.ones((B, t_att, KV_RANK), BF16)], axis=-1)
        o_p = _attn_prompt(q_p.transpose(0, 2, 1, 3), kt_p, v_p)

        q_s = qcat[n_pt:m0].reshape(DB, S, MLA_HEADS, dk).transpose(0, 2, 1, 3).reshape(DB, MLA_HEADS * S, dk)
        o_s = _attn_sample(page_table, q_s, sample_seqs(kcat, new_pad), cache_ckv[l],
                           jnp.swapaxes(cache_kpe[l], -1, -2), S)
        olat = flat_rows(o_p[:, :, N_META:T].transpose(0, 2, 1, 3),
                         o_s.reshape(DB, MLA_HEADS, S, KV_RANK).transpose(0, 2, 1, 3),
                         o_p[0, :, :N_META].transpose(1, 0, 2).reshape(N_META, MLA_HEADS * KV_RANK))

        rf_p, cf_p = gate_forms(prompt_seqs(gates, t_ml), T)
        h_p, C_p, nn_p, m_p = _mlstm(prompt_seqs(mq, t_ml), prompt_seqs(mk, t_ml), prompt_seqs(mv, t_ml), rf_p, cf_p,
                                     jnp.zeros((B, ML_HEADS, ML_DH, ML_DH), F32), jnp.zeros((B, ML_HEADS, ML_DH), F32),
                                     jnp.zeros((B, 1, ML_HEADS), F32))
        rf_s, cf_s = gate_forms(sample_seqs(gates, ML_CHUNK), S)
        h_s, C_s, nn_s, m_s = _mlstm(sample_seqs(mq, ML_CHUNK), sample_seqs(mk, ML_CHUNK), sample_seqs(mv, ML_CHUNK),
                                     rf_s, cf_s, state_C[l].astype(F32), state_n[l].astype(F32),
                                     state_m[l].astype(F32).reshape(DB, 1, ML_HEADS))
        hml = flat_rows(h_p[:, N_META:T], h_s[:, :S], h_p[0, :N_META])

        h1 = _mix(olat, hml, og, h, wuv, row(mh_norm_g[l]), w_o[l].astype(BF16), row(ln1_g[l]), row(ln1_b[l]), alpha)
        whi, wlo = _hi_lo(w_pq[l])
        khi, klo = _hi_lo(sub_keys[l])
        y = _peer(h1, whi, wlo, khi, klo, _pack_table(expert_u[l]), _pack_table(expert_v[l]))
        h = _res_ln(h1, y, row(ln2_g[l]), row(ln2_b[l]), alpha)
        h_main, h_small = h, h[n_pt:]

        outs["ckv_p"].append(prompt_seqs(ckv, T))
        outs["kpe_p"].append(prompt_seqs(kpe, T))
        outs["C_p"].append(C_p.astype(dt)); outs["n_p"].append(nn_p.astype(dt)); outs["m_p"].append(m_p.reshape(B, ML_HEADS).astype(dt))
        outs["ckv_s"].append(ckv[n_pt:m0].reshape(DB, S, KV_RANK))
        outs["kpe_s"].append(kpe[n_pt:m0].reshape(DB, S, QK_ROPE))
        outs["C_s"].append(C_s.astype(dt)); outs["n_s"].append(nn_s.astype(dt)); outs["m_s"].append(m_s.reshape(DB, ML_HEADS).astype(dt))

    y_prompt = h[:n_pt].reshape(B, seq, D)
    y_sample = h[n_pt:m0].reshape(DB, S, D)
    st = lambda k: jnp.stack(outs[k])
    return (y_prompt, y_sample, st("ckv_p"), st("kpe_p"), st("C_p"), st("n_p"), st("m_p"),
            st("ckv_s"), st("kpe_s"), st("C_s"), st("n_s"), st("m_s"))
```

```python
import functools

import jax
import jax.numpy as jnp
from jax import lax
from jax.experimental import pallas as pl
from jax.experimental.pallas import tpu as pltpu

F32 = jnp.float32
BF16 = jnp.bfloat16

N_META = 16
MLA_HEADS = 8
QK_NOPE = 64
QK_ROPE = 32
V_DIM = 64
Q_RANK = 256
KV_RANK = 128
ROPE_THETA = 10000.0
SM_SCALE = (QK_NOPE + QK_ROPE) ** -0.5
ML_HEADS = 4
ML_DH = 128
ML_W = ML_HEADS * ML_DH
MLA_W = MLA_HEADS * V_DIM
PEER_HEADS = 8
N_KEYS = 128
PEER_DQ = 256
PEER_TOPK = 16
EPS = 1e-5

LANES = 128
SUBLANES = 8
ROW_BLOCK = 256
ATT_TQ = 128
ATT_TK = 256
ML_CHUNK = 128
PAGES_PER_STEP = 16
PEER_TB = 128
PEER_PIPE = 8
NEG_BIG = -1e30
VMEM_LIMIT = 48 * 1024 * 1024

ZW = 2560
Z_CQ, Z_CKV, Z_SMALL, Z_MQ, Z_MK, Z_MV, Z_MO = 0, 256, 384, 512, 1024, 1536, 2048


def _dot(a, b):
    return jnp.dot(a, b, preferred_element_type=F32)


def _dot_nt(a, b):
    return lax.dot_general(a, b, (((1,), (1,)), ((), ())), preferred_element_type=F32)


def _dot_tn(a, b):
    return lax.dot_general(a, b, (((0,), (0,)), ((), ())), preferred_element_type=F32)


def _split3(x):
    a = x.astype(BF16)
    r = x - a.astype(F32)
    b = r.astype(BF16)
    c = (r - b.astype(F32)).astype(BF16)
    return a, b, c


def _dot_exact_rhs01(x, sel):
    a, b, c = _split3(x)
    return _dot(a, sel) + _dot(b, sel) + _dot(c, sel)


def _imod(i, n):
    if n & (n - 1) == 0:
        return jnp.bitwise_and(i, n - 1)
    f = i.astype(F32)
    return (f - n * jnp.floor(f / n)).astype(jnp.int32)


def _layer_norm(x, g, b):
    mu = jnp.mean(x, axis=-1, keepdims=True)
    xc = x - mu
    var = jnp.mean(xc * xc, axis=-1, keepdims=True)
    return xc * lax.rsqrt(var + EPS) * g + b


def _rms_norm(x, g):
    return x * lax.rsqrt(jnp.mean(x * x, axis=-1, keepdims=True) + 1e-6) * g


def _cparams(sem, vmem=VMEM_LIMIT):
    return pltpu.CompilerParams(dimension_semantics=sem, vmem_limit_bytes=vmem)


def _full(shape):
    n = len(shape)
    return pl.BlockSpec(shape, lambda *_: (0,) * n)


def _proj_kernel(xm_ref, xs_ref, cos_ref, sin_ref, lng_ref, lnb_ref, win_ref, qng_ref, wqb_ref, wuk_ref, kvg_ref,
                 h_ref, ckv_ref, kpe_ref, gates_ref, qlat_ref, qrope_ref, mq_ref, mk_ref, mv_ref, og_ref,
                 *, apply_ln, nb_main):
    x = jnp.where(pl.program_id(0) < nb_main, xm_ref[...], xs_ref[...])
    h = _layer_norm(x, lng_ref[...], lnb_ref[...]) if apply_ln else x
    h_ref[...] = h
    z = _dot(h.astype(BF16), win_ref[...])
    ckv_ref[...] = _rms_norm(z[:, Z_CKV:Z_SMALL], kvg_ref[...])
    q = _dot(_rms_norm(z[:, Z_CQ:Z_CKV], qng_ref[...]).astype(BF16), wqb_ref[...])
    nope_w = MLA_HEADS * QK_NOPE
    qlat_ref[...] = _dot(q[:, :nope_w].astype(BF16), wuk_ref[...]).astype(BF16)
    c = cos_ref[...]
    s = sin_ref[...]
    x1 = q[:, nope_w:nope_w + LANES]
    x2 = q[:, nope_w + LANES:nope_w + 2 * LANES]
    qrope_ref[:, 0:LANES] = (x1 * c - x2 * s).astype(BF16)
    qrope_ref[:, LANES:2 * LANES] = (x1 * s + x2 * c).astype(BF16)
    zs = z[:, Z_SMALL:Z_MQ]
    half = QK_ROPE // 2
    lane = lax.broadcasted_iota(jnp.int32, zs.shape, 1)
    first = lane < half
    partner = jnp.where(first, pltpu.roll(zs, LANES - half, 1), pltpu.roll(zs, half, 1))
    kpe = zs * c + partner * jnp.where(first, -s, s)
    kpe_ref[...] = kpe[:, 0:QK_ROPE]
    gates_ref[...] = zs
    mq_ref[...] = z[:, Z_MQ:Z_MK].astype(BF16)
    mk_ref[...] = (z[:, Z_MK:Z_MV] * (ML_DH ** -0.5)).astype(BF16)
    mv_ref[...] = z[:, Z_MV:Z_MO].astype(BF16)
    og_ref[...] = jax.nn.sigmoid(z[:, Z_MO:ZW])


def _proj(x_main, x_small, cos_t, sin_t, lng, lnb, win, qng, wqb, wuk, kvg, *, apply_ln):
    n, d = cos_t.shape[0], x_main.shape[1]
    tm = ROW_BLOCK
    nb_main = n // tm - x_small.shape[0] // tm
    rows = lambda w: pl.BlockSpec((tm, w), lambda i: (i, 0))
    out_w = [(d, F32), (KV_RANK, F32), (QK_ROPE, F32), (LANES, F32), (MLA_HEADS * KV_RANK, BF16),
             (2 * LANES, BF16), (ML_W, BF16), (ML_W, BF16), (ML_W, BF16), (ML_W, F32)]
    return pl.pallas_call(
        functools.partial(_proj_kernel, apply_ln=apply_ln, nb_main=nb_main),
        grid=(n // tm,),
        in_specs=[pl.BlockSpec((tm, d), lambda i: (jnp.minimum(i, nb_main - 1), 0)),
                  pl.BlockSpec((tm, d), lambda i: (jnp.maximum(i - nb_main, 0), 0)), rows(LANES), rows(LANES),
                  _full(lng.shape), _full(lnb.shape), _full(win.shape), _full(qng.shape), _full(wqb.shape),
                  _full(wuk.shape), _full(kvg.shape)],
        out_specs=[rows(w) for w, _ in out_w],
        out_shape=[jax.ShapeDtypeStruct((n, w), dt) for w, dt in out_w],
        compiler_params=_cparams(("parallel",)),
        name="proj",
    )(x_main, x_small, cos_t, sin_t, lng, lnb, win, qng, wqb, wuk, kvg)


def _attn_prompt_kernel(q_ref, kt_ref, v_ref, o_ref, m_sc, acc_sc):
    i = pl.program_id(1)
    nh, tq, dk = q_ref.shape[1:]
    tk = ATT_TK
    reps = tk // LANES
    m_sc[...] = jnp.full(m_sc.shape, -jnp.inf, F32)
    acc_sc[...] = jnp.zeros(acc_sc.shape, F32)
    q_pos = i * tq + lax.broadcasted_iota(jnp.int32, (tq, tk), 0)
    col = lax.broadcasted_iota(jnp.int32, (tq, tk), 1)
    n_full = (i * tq + 1) // tk

    def chunk(c, masked):
        kt = kt_ref[0, c]
        v = v_ref[0, pl.ds(pl.multiple_of(c * tk, tk), tk), :]
        for hd in range(nh):
            rows = slice(hd * tq, (hd + 1) * tq)
            s = _dot(q_ref[0, hd], kt) * SM_SCALE
            if masked:
                s = jnp.where(col + c * tk <= q_pos, s, -jnp.inf)
            m_prev = m_sc[rows, :]
            m_new = jnp.maximum(m_prev, jnp.max(s, axis=1, keepdims=True))
            alpha = jnp.exp(m_prev - m_new)
            p = jnp.exp(s - jnp.tile(m_new, (1, reps)))
            acc_sc[rows, :] = jnp.tile(alpha, (1, 2)) * acc_sc[rows, :] + _dot(p.astype(BF16), v)
            m_sc[rows, :] = m_new

    def body(c, carry):
        chunk(c, False)
        return carry

    lax.fori_loop(0, n_full, body, 0)
    chunk(n_full, True)
    for hd in range(nh):
        acc = acc_sc[hd * tq:(hd + 1) * tq, :]
        o_ref[0, hd] = (acc[:, :KV_RANK] / acc[:, KV_RANK:]).astype(o_ref.dtype)


def _attn_prompt(q, kt, v):
    b, nh, tp, dk = q.shape
    tq = ATT_TQ
    return pl.pallas_call(
        _attn_prompt_kernel,
        grid=(b, tp // tq),
        in_specs=[pl.BlockSpec((1, nh, tq, dk), lambda bi, i: (bi, 0, i, 0)),
                  pl.BlockSpec((1,) + kt.shape[1:], lambda bi, i: (bi, 0, 0, 0)),
                  pl.BlockSpec((1,) + v.shape[1:], lambda bi, i: (bi, 0, 0))],
        out_specs=pl.BlockSpec((1, nh, tq, KV_RANK), lambda bi, i: (bi, 0, i, 0)),
        out_shape=jax.ShapeDtypeStruct((b, nh, tp, KV_RANK), BF16),
        scratch_shapes=[pltpu.VMEM((nh * tq, LANES), F32), pltpu.VMEM((nh * tq, 2 * KV_RANK), F32)],
        compiler_params=_cparams(("parallel", "arbitrary")),
        name="attn_prompt",
    )(q, kt, v)


def _attn_sample_kernel(pt_ref, q_ref, knew_ref, *refs, n_new):
    pps = PAGES_PER_STEP
    ckv_refs, kpe_refs = refs[:pps], refs[pps:2 * pps]
    o_ref, m_sc, l_sc, acc_sc = refs[2 * pps:]
    g = pl.program_id(1)
    q = q_ref[0]
    ql, qr = q[:, :KV_RANK], q[:, KV_RANK:]

    @pl.when(g == 0)
    def _():
        m_sc[...] = jnp.full(m_sc.shape, -jnp.inf, F32)
        l_sc[...] = jnp.zeros(l_sc.shape, F32)
        acc_sc[...] = jnp.zeros(acc_sc.shape, F32)

    def update(s, vals):
        m_prev = m_sc[...]
        m_new = jnp.maximum(m_prev, jnp.max(s, axis=1, keepdims=True))
        alpha = jnp.exp(m_prev - m_new)
        p = jnp.exp(s - m_new)
        l_sc[...] = alpha * l_sc[...] + jnp.sum(p, axis=1, keepdims=True)
        acc = alpha * acc_sc[...]
        w = p.shape[1] // len(vals)
        for j, v in enumerate(vals):
            acc = acc + _dot(p[:, j * w:(j + 1) * w].astype(BF16), v)
        acc_sc[...] = acc
        m_sc[...] = m_new

    kc = [r[...].astype(BF16) for r in ckv_refs]
    s = jnp.concatenate([_dot_nt(ql, kc[j]) + _dot(qr, kpe_refs[j][...].astype(BF16)) for j in range(pps)],
                        axis=1) * SM_SCALE
    update(s, kc)

    @pl.when(g == pl.num_programs(1) - 1)
    def _():
        kn = knew_ref[0]
        s2 = _dot_nt(q, kn) * SM_SCALE
        row = lax.broadcasted_iota(jnp.int32, s2.shape, 0)
        colk = lax.broadcasted_iota(jnp.int32, s2.shape, 1)
        s2 = jnp.where((colk <= _imod(row, n_new)) & (colk < n_new), s2, -jnp.inf)
        update(s2, [kn[:, :KV_RANK]])
        o_ref[0] = (acc_sc[...] / l_sc[...]).astype(o_ref.dtype)


def _attn_sample(page_table, q, knew, cache_ckv, cache_kpe_t, n_new):
    db, r, dk = q.shape
    n_pages = page_table.shape[1]
    pps = PAGES_PER_STEP
    newp = knew.shape[1]

    def page_spec(arr, j):
        return pl.BlockSpec((None,) + arr.shape[1:], lambda b, g, pt: (pt[b, g * pps + j], 0, 0))

    grid_spec = pltpu.PrefetchScalarGridSpec(
        num_scalar_prefetch=1,
        grid=(db, n_pages // pps),
        in_specs=[pl.BlockSpec((1, r, dk), lambda b, g, pt: (b, 0, 0)),
                  pl.BlockSpec((1, newp, dk), lambda b, g, pt: (b, 0, 0))]
                 + [page_spec(cache_ckv, j) for j in range(pps)]
                 + [page_spec(cache_kpe_t, j) for j in range(pps)],
        out_specs=pl.BlockSpec((1, r, KV_RANK), lambda b, g, pt: (b, 0, 0)),
        scratch_shapes=[pltpu.VMEM((r, 1), F32), pltpu.VMEM((r, 1), F32), pltpu.VMEM((r, KV_RANK), F32)],
    )
    return pl.pallas_call(
        functools.partial(_attn_sample_kernel, n_new=n_new),
        grid_spec=grid_spec,
        out_shape=jax.ShapeDtypeStruct((db, r, KV_RANK), BF16),
        compiler_params=_cparams(("parallel", "arbitrary")),
        name="attn_sample",
    )(page_table, q, knew, *([cache_ckv] * pps), *([cache_kpe_t] * pps))


def _gate_prep_kernel(g_ref, bias_ref, valid_ref, triu_ref, o_ref):
    x = g_ref[...] + bias_ref[...]
    valid = valid_ref[...] > 0.5
    is_i = lax.broadcasted_iota(jnp.int32, x.shape, 1) < ML_HEADS
    lf = jnp.minimum(x, 0.0) - jnp.log1p(jnp.exp(-jnp.abs(x)))
    y = jnp.where(is_i, jnp.where(valid, x, NEG_BIG), jnp.where(valid, lf, 0.0))
    r, g8, l = x.shape
    cum = _dot_exact_rhs01(y.reshape(r * g8, l), triu_ref[...]).reshape(r, g8, l)
    o_ref[...] = jnp.where(is_i, y, cum)


def _gate_prep(gt, bias, valid):
    r, g8, l = gt.shape
    rb = 8 if r % 8 == 0 else r
    triu = (jnp.arange(l)[:, None] <= jnp.arange(l)[None, :]).astype(BF16)
    return pl.pallas_call(
        _gate_prep_kernel,
        grid=(r // rb,),
        in_specs=[pl.BlockSpec((rb, g8, l), lambda i: (i, 0, 0)), _full(bias.shape),
                  pl.BlockSpec((rb, 1, l), lambda i: (i, 0, 0)), _full(triu.shape)],
        out_specs=pl.BlockSpec((rb, g8, l), lambda i: (i, 0, 0)),
        out_shape=jax.ShapeDtypeStruct((r, g8, l), F32),
        compiler_params=_cparams(("parallel",)),
        name="gate_prep",
    )(gt, bias, valid, triu)


def _mlstm_kernel(q_ref, k_ref, v_ref, rf_ref, cf_ref, c0_ref, n0_ref, m0_ref,
                  h_ref, c_ref, n_ref, m_ref, c_sc, n_sc, m_sc):
    n_chunks = rf_ref.shape[1]
    lc = ML_CHUNK
    c_sc[...] = c0_ref[0]
    n_sc[...] = n0_ref[0]
    m_sc[...] = m0_ref[0]
    row = lax.broadcasted_iota(jnp.int32, (lc, lc), 0)
    colm = lax.broadcasted_iota(jnp.int32, (lc, lc), 1)
    causal = row >= colm

    def body(ci, carry):
        start = pl.multiple_of(ci * lc, lc)
        rf = rf_ref[0, ci]
        cf = cf_ref[0, ci]
        for hd in range(ML_HEADS):
            sl = slice(hd * ML_DH, (hd + 1) * ML_DH)
            q = q_ref[0, pl.ds(start, lc), sl]
            k = k_ref[0, pl.ds(start, lc), sl]
            v = v_ref[0, pl.ds(start, lc), sl]
            ig_row, b_row = rf[hd:hd + 1, :], rf[ML_HEADS + hd:ML_HEADS + hd + 1, :]
            ig_col, b_col = cf[:, hd:hd + 1], cf[:, ML_HEADS + hd:ML_HEADS + hd + 1]
            c0 = c_sc[hd]
            n0 = n_sc[hd:hd + 1, :]
            m0 = m_sc[0:1, hd:hd + 1]
            d = jnp.where(causal, b_col - b_row + ig_row, -jnp.inf)
            inter = b_col + m0
            m = jnp.maximum(inter, jnp.max(d, axis=1, keepdims=True))
            w = jnp.exp(d - m)
            a = jnp.exp(inter - m)
            wqk = w * _dot_nt(q, k)
            num = a * _dot_nt(q, c0.astype(BF16)) + _dot(wqk.astype(BF16), v)
            qf = q.astype(F32)
            den = a * jnp.sum(qf * n0, axis=1, keepdims=True) + jnp.sum(wqk, axis=1, keepdims=True)
            h_ref[0, pl.ds(start, lc), sl] = num / jnp.maximum(jnp.abs(den), jnp.exp(-m))
            b_last = b_col[lc - 1:lc, :]
            dec_row = b_last - b_row + ig_row
            m_last = jnp.maximum(b_last + m0, jnp.max(dec_row, axis=1, keepdims=True))
            ws_col = jnp.exp(b_last - b_col + ig_col - m_last)
            a_last = jnp.exp(b_last + m0 - m_last)
            c_sc[hd] = a_last * c0 + _dot_tn((v.astype(F32) * ws_col).astype(BF16), k)
            n_sc[hd:hd + 1, :] = a_last * n0 + jnp.sum(ws_col * k.astype(F32), axis=0, keepdims=True)
            m_sc[0:1, hd:hd + 1] = m_last
        return carry

    lax.fori_loop(0, n_chunks, body, 0)
    c_ref[0] = c_sc[...]
    n_ref[0] = n_sc[...]
    m_ref[0] = m_sc[...]


def _mlstm(q, k, v, rf, cf, c0, n0, m0):
    b, t, w = q.shape
    nc = rf.shape[1]
    seq = pl.BlockSpec((1, t, w), lambda i: (i, 0, 0))
    st = [pl.BlockSpec((1,) + c0.shape[1:], lambda i: (i, 0, 0, 0)),
          pl.BlockSpec((1,) + n0.shape[1:], lambda i: (i, 0, 0)),
          pl.BlockSpec((1,) + m0.shape[1:], lambda i: (i, 0, 0))]
    return pl.pallas_call(
        _mlstm_kernel,
        grid=(b,),
        in_specs=[seq, seq, seq,
                  pl.BlockSpec((1,) + rf.shape[1:], lambda i: (i, 0, 0, 0)),
                  pl.BlockSpec((1,) + cf.shape[1:], lambda i: (i, 0, 0, 0))] + st,
        out_specs=[seq] + st,
        out_shape=[jax.ShapeDtypeStruct((b, t, w), F32), jax.ShapeDtypeStruct(c0.shape, F32),
                   jax.ShapeDtypeStruct(n0.shape, F32), jax.ShapeDtypeStruct(m0.shape, F32)],
        scratch_shapes=[pltpu.VMEM(c0.shape[1:], F32), pltpu.VMEM(n0.shape[1:], F32), pltpu.VMEM(m0.shape[1:], F32)],
        compiler_params=_cparams(("parallel",)),
        name="mlstm",
    )(q, k, v, rf, cf, c0, n0, m0)


def _mix_kernel(olat_ref, hml_ref, og_ref, res_ref, wuv_ref, mhg_ref, wo_ref, g_ref, b_ref, o_ref, o8_ref, *, alpha):
    omla = _dot(olat_ref[...], wuv_ref[...])
    hml = hml_ref[...]
    parts = []
    for hd in range(ML_HEADS):
        x = hml[:, hd * ML_DH:(hd + 1) * ML_DH]
        mu = jnp.mean(x, axis=-1, keepdims=True)
        xc = x - mu
        var = jnp.mean(xc * xc, axis=-1, keepdims=True)
        parts.append(xc * lax.rsqrt(var + EPS))
    oml = (jnp.concatenate(parts, axis=1) * mhg_ref[...]) * og_ref[...]
    cat = jnp.concatenate([omla.astype(BF16), oml.astype(BF16)], axis=1)
    a = _dot(cat, wo_ref[...])
    h1 = _layer_norm(alpha * res_ref[...] + a, g_ref[...], b_ref[...])
    o_ref[...] = h1
    tm = h1.shape[0]
    for c in range(h1.shape[1] // LANES):
        o8_ref[pl.ds(c, tm, stride=SUBLANES), :] = h1[:, c * LANES:(c + 1) * LANES]


def _mix(olat, hml, og, res, wuv, mhg, wo, g, b, alpha):
    n, d = res.shape
    tm = ROW_BLOCK
    rows = lambda w: pl.BlockSpec((tm, w), lambda i: (i, 0))
    return pl.pallas_call(
        functools.partial(_mix_kernel, alpha=alpha),
        grid=(n // tm,),
        in_specs=[rows(olat.shape[1]), rows(ML_W), rows(ML_W), rows(d), _full(wuv.shape), _full(mhg.shape),
                  _full(wo.shape), _full(g.shape), _full(b.shape)],
        out_specs=[rows(d), pl.BlockSpec((tm * SUBLANES, LANES), lambda i: (i, 0))],
        out_shape=[jax.ShapeDtypeStruct((n, d), F32), jax.ShapeDtypeStruct((n * SUBLANES, LANES), F32)],
        compiler_params=_cparams(("parallel",)),
        name="mix",
    )(olat, hml, og, res, wuv, mhg, wo, g, b)


def _res_ln_kernel(res_ref, y8_ref, g_ref, b_ref, o_ref, *, alpha):
    tm, d = res_ref.shape
    y = jnp.concatenate([y8_ref[pl.ds(c, tm, stride=SUBLANES), :] for c in range(d // LANES)], axis=1)
    o_ref[...] = _layer_norm(alpha * res_ref[...] + y, g_ref[...], b_ref[...])


def _res_ln(res, y, g, b, alpha):
    n, d = res.shape
    tm = ROW_BLOCK
    rows = pl.BlockSpec((tm, d), lambda i: (i, 0))
    return pl.pallas_call(
        functools.partial(_res_ln_kernel, alpha=alpha),
        grid=(n // tm,),
        in_specs=[rows, pl.BlockSpec((tm * SUBLANES, LANES), lambda i: (i, 0)), _full(g.shape), _full(b.shape)],
        out_specs=rows,
        out_shape=jax.ShapeDtypeStruct((n, d), F32),
        compiler_params=_cparams(("parallel",)),
        name="res_ln",
    )(res, y, g, b)


def _topk_rounds(jobs):
    def body(kk, carry):
        for s_sc, n_rows, tie, vals_sc, idx_sc in jobs:
            s = s_sc[0:n_rows, :]
            m = jnp.max(s, axis=0, keepdims=True)
            i = jnp.min(jnp.where(s == m, tie, jnp.inf), axis=0, keepdims=True)
            vals_sc[pl.ds(kk, 1), :] = m
            idx_sc[pl.ds(kk, 1), :] = i
            s_sc[0:n_rows, :] = jnp.where(tie == i, -jnp.inf, s)
        return carry

    lax.fori_loop(0, PEER_TOPK, body, 0)


def _candidate_blocks():
    k, sub = PEER_TOPK, SUBLANES
    need = [k // (i + 1) for i in range(k)]
    blocks, covered = [], set()
    for i in range(k):
        if need[i] >= sub:
            nj = -(-need[i] // sub) * sub
            blocks.append(("row", i, nj))
            covered |= {(i, j) for j in range(nj)}
    for i0 in range(0, k, sub):
        rest = [i for i in range(i0, i0 + sub) if need[i] < sub]
        if rest:
            for j in range(max(need[i] for i in rest)):
                blocks.append(("col", j, i0, rest[0]))
                covered |= {(i, j) for i in range(rest[0], i0 + sub)}
    assert all((i, j) in covered for i in range(k) for j in range(k) if (i + 1) * (j + 1) <= k)
    return blocks


CAND_BLOCKS = _candidate_blocks()
CAND_ROWS = sum(b[2] if b[0] == "row" else SUBLANES for b in CAND_BLOCKS)


def _route_kernel(x_ref, whi_ref, wlo_ref, khi_ref, klo_ref, idx_ref, g_ref,
                  q_sc, s0_sc, s1_sc, c_sc, sv0, si0, sv1, si1, cv, ci):
    hd = pl.program_id(1)
    kq = PEER_DQ // 2
    tm = x_ref.shape[0]

    @pl.when(hd == 0)
    def _():
        x = x_ref[...]
        xh = x.astype(BF16)
        xl = (x - xh.astype(F32)).astype(BF16)
        q = _dot(xh, whi_ref[...]) + _dot(xh, wlo_ref[...]) + _dot(xl, whi_ref[...])
        for j in range(PEER_HEADS):
            q_sc[j] = q[:, j * PEER_DQ:(j + 1) * PEER_DQ]

    qh_all = q_sc[hd]
    for p, s_sc in enumerate((s0_sc, s1_sc)):
        qq = qh_all[:, p * kq:(p + 1) * kq]
        qh = qq.astype(BF16)
        ql = (qq - qh.astype(F32)).astype(BF16)
        kh, kl = khi_ref[0, p], klo_ref[0, p]
        s_sc[...] = _dot_nt(kh, qh) + _dot_nt(kh, ql) + _dot_nt(kl, qh)
    key_iota = lax.broadcasted_iota(jnp.int32, (N_KEYS, tm), 0).astype(F32)
    _topk_rounds([(s0_sc, N_KEYS, key_iota, sv0, si0), (s1_sc, N_KEYS, key_iota, sv1, si1)])

    row_iota = lambda n: lax.broadcasted_iota(jnp.int32, (n, tm), 0).astype(F32)
    flat_parts, off = [], 0
    for blk in CAND_BLOCKS:
        if blk[0] == "row":
            _, i, nj = blk
            c_sc[off:off + nj, :] = sv0[i:i + 1, :] + sv1[0:nj, :]
            flat_parts.append(row_iota(nj) + float(i * PEER_TOPK))
            off += nj
        else:
            _, j, i0, imin = blk
            ii = row_iota(SUBLANES) + float(i0)
            c_sc[off:off + SUBLANES, :] = jnp.where(ii >= float(imin), sv0[i0:i0 + SUBLANES, :] + sv1[j:j + 1, :], -jnp.inf)
            flat_parts.append(ii * float(PEER_TOPK) + float(j))
            off += SUBLANES
    _topk_rounds([(c_sc, CAND_ROWS, jnp.concatenate(flat_parts, axis=0), cv, ci)])
    cidx = ci[...]
    i1 = jnp.floor(cidx * (1.0 / PEER_TOPK))
    i2 = cidx - i1 * PEER_TOPK
    e1 = jnp.zeros_like(cidx)
    e2 = jnp.zeros_like(cidx)
    for i in range(PEER_TOPK):
        e1 = jnp.where(i1 == float(i), si0[i:i + 1, :], e1)
        e2 = jnp.where(i2 == float(i), si1[i:i + 1, :], e2)
    idx_ref[0] = (e1 * float(N_KEYS) + e2).astype(jnp.int32)
    cvv = cv[...]
    e = jnp.exp(cvv - cvv[0:1, :])
    g_ref[0] = e / jnp.sum(e, axis=0, keepdims=True)


def _route(x, whi, wlo, khi, klo):
    n, d = x.shape
    tm = ROW_BLOCK
    out_spec = pl.BlockSpec((1, PEER_TOPK, tm), lambda i, h: (h, 0, i))
    key_spec = pl.BlockSpec((1, 2, N_KEYS, PEER_DQ // 2), lambda i, h: (h, 0, 0, 0))
    tk = pltpu.VMEM((PEER_TOPK, tm), F32)
    return pl.pallas_call(
        _route_kernel,
        grid=(n // tm, PEER_HEADS),
        in_specs=[pl.BlockSpec((tm, d), lambda i, h: (i, 0)), _full(whi.shape), _full(wlo.shape), key_spec, key_spec],
        out_specs=[out_spec, out_spec],
        out_shape=[jax.ShapeDtypeStruct((PEER_HEADS, PEER_TOPK, n), jnp.int32),
                   jax.ShapeDtypeStruct((PEER_HEADS, PEER_TOPK, n), F32)],
        scratch_shapes=[pltpu.VMEM((PEER_HEADS, tm, PEER_DQ), F32), pltpu.VMEM((N_KEYS, tm), F32),
                        pltpu.VMEM((N_KEYS, tm), F32), pltpu.VMEM((CAND_ROWS, tm), F32), tk, tk, tk, tk, tk, tk],
        compiler_params=_cparams(("parallel", "arbitrary")),
        name="peer_route",
    )(x, whi, wlo, khi, klo)


ROW_WORDS = 4
SLOTS = PEER_HEADS * PEER_TOPK


def _gather_rows(idx_ref, tab_ref, g_sc, t):
    grp = 8
    for g in range(SLOTS // grp):
        idx_g = idx_ref.at[pl.ds(t * SLOTS + g * grp, grp)]
        for j in range(grp):
            r = g * grp + j
            e4 = pl.multiple_of(idx_g[j], ROW_WORDS)
            g_sc[pl.ds(ROW_WORDS * r, ROW_WORDS), :] = tab_ref[pl.ds(e4, ROW_WORDS), :]


def _token_pipeline(tb, depth, gather, contract, finish):
    for b in range(depth):
        gather(b, b)

    def body(i, carry):
        t = depth * i
        res = [contract(t + b, b) for b in range(depth)]
        for b in range(depth):
            gather(jnp.minimum(t + depth + b, tb - 1), b)
        for b in range(depth):
            finish(t + b, res[b])
        return carry

    lax.fori_loop(0, tb // depth, body, 0)


def _diag_mask():
    rows16 = 2 * SUBLANES
    sub = lax.broadcasted_iota(jnp.int32, (rows16, SUBLANES * SLOTS), 0)
    lane = lax.broadcasted_iota(jnp.int32, (rows16, SUBLANES * SLOTS), 1)
    return _imod(lane, SUBLANES) == _imod(sub, SUBLANES), sub < SUBLANES


def _hi_lo_rows(x8):
    hi = x8.astype(BF16).astype(F32)
    return jnp.concatenate([hi, x8 - hi], axis=0).astype(BF16)


def _peer_u_kernel(idx_ref, x_ref, tab_ref, sel_ref, act_ref, *scratch):
    bufs, d_sc = scratch[:-1], scratch[-1]
    tb = x_ref.shape[0]
    diag, _ = _diag_mask()

    def gather(t, b):
        _gather_rows(idx_ref, tab_ref, bufs[b], t)

    def contract(t, b):
        gb = pltpu.bitcast(bufs[b][...], BF16)
        return _dot_nt(_hi_lo_rows(x_ref[t]), gb)

    def finish(t, d):
        d_sc[pl.ds(t, 1), :] = jnp.sum(jnp.where(diag, d, 0.0), axis=0, keepdims=True)

    _token_pipeline(tb, len(bufs), gather, contract, finish)
    act_ref[...] = _dot_exact_rhs01(d_sc[...], sel_ref[...])


def _peer_v_kernel(idx_ref, act_ref, gate_ref, tab_ref, selt_ref, o_ref, *scratch):
    bufs, w_sc = scratch[:-1], scratch[-1]
    tb = act_ref.shape[0]
    w = gate_ref[...] * jax.nn.gelu(act_ref[...], approximate=True)
    w_sc[...] = _dot_exact_rhs01(w, selt_ref[...])
    diag, top = _diag_mask()

    def gather(t, b):
        _gather_rows(idx_ref, tab_ref, bufs[b], t)

    def contract(t, b):
        gb = pltpu.bitcast(bufs[b][...], BF16)
        w16 = jnp.where(diag, jnp.broadcast_to(w_sc[pl.ds(t, 1), :], diag.shape), 0.0)
        hi = w16.astype(BF16).astype(F32)
        lhs = jnp.where(top, hi, w16 - hi).astype(BF16)
        return _dot(lhs, gb)

    def finish(t, o16):
        o_ref[pl.ds(pl.multiple_of(t * SUBLANES, SUBLANES), SUBLANES), :] = o16[0:SUBLANES] + o16[SUBLANES:]

    _token_pipeline(tb, len(bufs), gather, contract, finish)


def _table_spec(shape):
    return pl.BlockSpec(shape, lambda i: (0, 0), pipeline_mode=pl.Buffered(1))


def _peer_scratch(tb):
    gbuf = pltpu.VMEM((ROW_WORDS * SLOTS, LANES), jnp.uint32)
    return [gbuf] * PEER_PIPE + [pltpu.VMEM((tb, SUBLANES * SLOTS), F32)]


def _peer_u(idx4, x8, tab, sel):
    n = x8.shape[0]
    tb = PEER_TB
    return pl.pallas_call(
        _peer_u_kernel,
        grid=(n // tb,),
        in_specs=[pl.BlockSpec((tb * SLOTS,), lambda i: (i,), memory_space=pltpu.SMEM),
                  pl.BlockSpec((tb, SUBLANES, LANES), lambda i: (i, 0, 0)),
                  _table_spec(tab.shape), _full(sel.shape)],
        out_specs=pl.BlockSpec((tb, SLOTS), lambda i: (i, 0)),
        out_shape=jax.ShapeDtypeStruct((n, SLOTS), F32),
        scratch_shapes=_peer_scratch(tb),
        compiler_params=_cparams(("parallel",), vmem=56 * 1024 * 1024),
        name="peer_u",
    )(idx4, x8, tab, sel)


def _peer_v(idx4, act, gate, tab, selt):
    n = act.shape[0]
    tb = PEER_TB
    return pl.pallas_call(
        _peer_v_kernel,
        grid=(n // tb,),
        in_specs=[pl.BlockSpec((tb * SLOTS,), lambda i: (i,), memory_space=pltpu.SMEM),
                  pl.BlockSpec((tb, SLOTS), lambda i: (i, 0)), pl.BlockSpec((tb, SLOTS), lambda i: (i, 0)),
                  _table_spec(tab.shape), _full(selt.shape)],
        out_specs=pl.BlockSpec((tb * SUBLANES, LANES), lambda i: (i, 0)),
        out_shape=jax.ShapeDtypeStruct((n * SUBLANES, LANES), F32),
        scratch_shapes=_peer_scratch(tb),
        compiler_params=_cparams(("parallel",), vmem=56 * 1024 * 1024),
        name="peer_v",
    )(idx4, act, gate, tab, selt)


def _pack_kernel(t_ref, o_ref, tile_sc):
    r, d = t_ref.shape
    x = t_ref[...]
    for c in range(d // LANES):
        tile_sc[pl.ds(c, r, stride=SUBLANES), :] = x[:, c * LANES:(c + 1) * LANES]
    o_ref[...] = pltpu.bitcast(tile_sc[...].astype(BF16), jnp.uint32)


def _pack_table(t):
    e, d = t.shape
    rb = ROW_BLOCK
    return pl.pallas_call(
        _pack_kernel,
        grid=(e // rb,),
        in_specs=[pl.BlockSpec((rb, d), lambda i: (i, 0))],
        out_specs=pl.BlockSpec((rb * ROW_WORDS, LANES), lambda i: (i, 0)),
        out_shape=jax.ShapeDtypeStruct((e * ROW_WORDS, LANES), jnp.uint32),
        scratch_shapes=[pltpu.VMEM((rb * d // LANES, LANES), F32)],
        compiler_params=_cparams(("parallel",)),
        name="pack_table",
    )(t)


def _peer(x, x8, whi, wlo, khi, klo, tab_u, tab_v):
    n, d = x.shape
    idx_t, gate_t = _route(x, whi, wlo, khi, klo)
    idx4 = (idx_t.reshape(SLOTS, n).T * ROW_WORDS).reshape(n * SLOTS)
    gate = gate_t.reshape(SLOTS, n).T
    grp = jnp.arange(SUBLANES * SLOTS) // SUBLANES
    sel = (grp[:, None] == jnp.arange(SLOTS)[None, :]).astype(BF16)
    act = _peer_u(idx4, x8.reshape(n, SUBLANES, LANES), tab_u, sel)
    return _peer_v(idx4, act, gate, tab_v, sel.T)


def _pad_rows(x, n):
    return jnp.pad(x, ((0, n - x.shape[0]),) + ((0, 0),) * (x.ndim - 1))


def _rope_tables(pos):
    half = QK_ROPE // 2
    inv = ROPE_THETA ** (-jnp.arange(half, dtype=F32) / half)
    ang = pos.astype(F32)[:, None] * inv
    reps = LANES // half
    return jnp.tile(jnp.cos(ang), (1, reps)), jnp.tile(jnp.sin(ang), (1, reps))


def _layer_weights(w_in, w_qb, w_kvb):
    d = w_in.shape[0]
    o_ckv = Q_RANK
    o_kr = o_ckv + KV_RANK
    o_mq = o_kr + QK_ROPE
    o_g = o_mq + 4 * ML_W
    small = jnp.concatenate([w_in[:, o_kr:o_mq], w_in[:, o_g:o_g + 2 * ML_HEADS],
                             jnp.zeros((d, LANES - QK_ROPE - 2 * ML_HEADS), w_in.dtype)], axis=1)
    win = jnp.concatenate([w_in[:, :o_kr], small, w_in[:, o_mq:o_g]], axis=1).astype(BF16)
    half = QK_ROPE // 2
    wq = w_qb.reshape(Q_RANK, MLA_HEADS, QK_NOPE + QK_ROPE)
    wqb = jnp.concatenate([wq[..., :QK_NOPE].reshape(Q_RANK, -1), wq[..., QK_NOPE:QK_NOPE + half].reshape(Q_RANK, -1),
                           wq[..., QK_NOPE + half:].reshape(Q_RANK, -1)], axis=1).astype(BF16)
    wkv = w_kvb.reshape(KV_RANK, MLA_HEADS, QK_NOPE + V_DIM)
    eye = jnp.eye(MLA_HEADS, dtype=w_kvb.dtype)
    wuk = jnp.einsum('chd,hg->hdgc', wkv[..., :QK_NOPE], eye).reshape(MLA_HEADS * QK_NOPE, MLA_HEADS * KV_RANK)
    wuv = jnp.einsum('chv,hg->hcgv', wkv[..., QK_NOPE:], eye).reshape(MLA_HEADS * KV_RANK, MLA_W)
    return win, wqb, wuk.astype(BF16), wuv.astype(BF16)


def _hi_lo(w):
    hi = w.astype(BF16)
    return hi, (w - hi.astype(F32)).astype(BF16)


def kernel(x_prompt, x_sample, cache_ckv, cache_kpe, page_table, state_C, state_n, state_m, meta_tokens,
           ln_in_g, ln_in_b, w_in, b_gates, q_norm_g, w_qb, kv_norm_g, w_kvb, mh_norm_g, w_o, ln1_g, ln1_b,
           w_pq, sub_keys, expert_u, expert_v, ln2_g, ln2_b):
    B, seq, D = x_prompt.shape
    DB, S = x_sample.shape[:2]
    depth = w_in.shape[0]
    dt = x_prompt.dtype
    T = seq + N_META
    page = cache_ckv.shape[2]
    past_len = page_table.shape[1] * page
    alpha = (2.0 * depth) ** 0.25
    row = lambda a: a.reshape(1, -1)

    n_pt, n_s = B * seq, DB * S
    assert n_pt % ROW_BLOCK == 0, "prompt rows must fill whole row blocks"
    m0 = n_pt + n_s
    n_small = -(-(n_s + N_META) // ROW_BLOCK) * ROW_BLOCK
    n_pad = n_pt + n_small
    x_small = _pad_rows(jnp.concatenate([x_sample.reshape(n_s, D), meta_tokens.astype(dt)], axis=0), n_small)
    pos = jnp.concatenate([jnp.tile(N_META + jnp.arange(seq), B), jnp.tile(past_len + jnp.arange(S), DB),
                           jnp.arange(N_META), jnp.zeros((n_pad - m0 - N_META,), jnp.int32)])
    cos_t, sin_t = _rope_tables(pos)

    t_att = -(-T // ATT_TK) * ATT_TK
    nc_p = -(-T // ML_CHUNK)
    t_ml = nc_p * ML_CHUNK
    new_pad = 16

    def prompt_seqs(a, t_total):
        w = a.shape[-1]
        parts = [jnp.broadcast_to(a[m0:m0 + N_META][None], (B, N_META, w)), a[:n_pt].reshape(B, seq, w)]
        if t_total > T:
            parts.append(jnp.zeros((B, t_total - T, w), a.dtype))
        return jnp.concatenate(parts, axis=1)

    def sample_seqs(a, t_total):
        return jnp.pad(a[n_pt:m0].reshape(DB, S, a.shape[-1]), ((0, 0), (0, t_total - S), (0, 0)))

    def flat_rows(prompt_part, sample_part, meta_part):
        w = meta_part.shape[-1]
        return _pad_rows(jnp.concatenate([prompt_part.reshape(n_pt, w), sample_part.reshape(n_s, w), meta_part], axis=0),
                         n_pad)

    def gate_forms(g_seq, n_valid):
        nb, t_total = g_seq.shape[:2]
        nc = t_total // ML_CHUNK
        gp = g_seq.reshape(nb * nc, ML_CHUNK, 2 * ML_HEADS).transpose(0, 2, 1)
        valid = jnp.broadcast_to((jnp.arange(t_total) < n_valid).astype(F32)[None], (nb, t_total)).reshape(nb * nc, 1, ML_CHUNK)
        rf = _gate_prep(gp, bias, valid).reshape(nb, nc, 2 * ML_HEADS, ML_CHUNK)
        return rf, rf.transpose(0, 1, 3, 2)

    outs = {k: [] for k in ("ckv_p", "kpe_p", "C_p", "n_p", "m_p", "ckv_s", "kpe_s", "C_s", "n_s", "m_s")}
    h_main, h_small = x_prompt.reshape(n_pt, D), x_small
    for l in range(depth):
        win, wqb, wuk, wuv = _layer_weights(w_in[l], w_qb[l], w_kvb[l])
        (h, ckv, kpe, zs, qlat, qrope, mq, mk, mv, og) = _proj(
            h_main, h_small, cos_t, sin_t, row(ln_in_g), row(ln_in_b), win, row(q_norm_g[l]), wqb, wuk,
            row(kv_norm_g[l]), apply_ln=(l == 0))
        gates = zs[:, QK_ROPE:QK_ROPE + 2 * ML_HEADS]
        bias = b_gates[l].astype(F32).reshape(1, 2 * ML_HEADS, 1)

        half = QK_ROPE // 2
        dk = KV_RANK + QK_ROPE
        qcat = jnp.concatenate([qlat.reshape(n_pad, MLA_HEADS, KV_RANK),
                                qrope[:, :LANES].reshape(n_pad, MLA_HEADS, half),
                                qrope[:, LANES:].reshape(n_pad, MLA_HEADS, half)], axis=-1)
        kcat = jnp.concatenate([ckv, kpe], axis=-1).astype(BF16)

        q_p = prompt_seqs(qcat.reshape(n_pad, MLA_HEADS * dk), t_att).reshape(B, t_att, MLA_HEADS, dk)
        k_p = prompt_seqs(kcat, t_att)
        kt_p = k_p.reshape(B, t_att // ATT_TK, ATT_TK, dk).transpose(0, 1, 3, 2)
        v_p = jnp.concatenate([k_p[..., :KV_RANK], jnp.ones((B, t_att, KV_RANK), BF16)], axis=-1)
        o_p = _attn_prompt(q_p.transpose(0, 2, 1, 3), kt_p, v_p)

        q_s = qcat[n_pt:m0].reshape(DB, S, MLA_HEADS, dk).transpose(0, 2, 1, 3).reshape(DB, MLA_HEADS * S, dk)
        o_s = _attn_sample(page_table, q_s, sample_seqs(kcat, new_pad), cache_ckv[l],
                           jnp.swapaxes(cache_kpe[l], -1, -2), S)
        olat = flat_rows(o_p[:, :, N_META:T].transpose(0, 2, 1, 3),
                         o_s.reshape(DB, MLA_HEADS, S, KV_RANK).transpose(0, 2, 1, 3),
                         o_p[0, :, :N_META].transpose(1, 0, 2).reshape(N_META, MLA_HEADS * KV_RANK))

        rf_p, cf_p = gate_forms(prompt_seqs(gates, t_ml), T)
        h_p, C_p, nn_p, m_p = _mlstm(prompt_seqs(mq, t_ml), prompt_seqs(mk, t_ml), prompt_seqs(mv, t_ml), rf_p, cf_p,
                                     jnp.zeros((B, ML_HEADS, ML_DH, ML_DH), F32), jnp.zeros((B, ML_HEADS, ML_DH), F32),
                                     jnp.zeros((B, 1, ML_HEADS), F32))
        rf_s, cf_s = gate_forms(sample_seqs(gates, ML_CHUNK), S)
        h_s, C_s, nn_s, m_s = _mlstm(sample_seqs(mq, ML_CHUNK), sample_seqs(mk, ML_CHUNK), sample_seqs(mv, ML_CHUNK),
                                     rf_s, cf_s, state_C[l].astype(F32), state_n[l].astype(F32),
                                     state_m[l].astype(F32).reshape(DB, 1, ML_HEADS))
        hml = flat_rows(h_p[:, N_META:T], h_s[:, :S], h_p[0, :N_META])

        h1, h1_tiles = _mix(olat, hml, og, h, wuv, row(mh_norm_g[l]), w_o[l].astype(BF16), row(ln1_g[l]), row(ln1_b[l]), alpha)
        whi, wlo = _hi_lo(w_pq[l])
        khi, klo = _hi_lo(sub_keys[l])
        y = _peer(h1, h1_tiles, whi, wlo, khi, klo, _pack_table(expert_u[l]), _pack_table(expert_v[l]))
        h = _res_ln(h1, y, row(ln2_g[l]), row(ln2_b[l]), alpha)
        h_main, h_small = h, h[n_pt:]

        outs["ckv_p"].append(prompt_seqs(ckv, T))
        outs["kpe_p"].append(prompt_seqs(kpe, T))
        outs["C_p"].append(C_p.astype(dt)); outs["n_p"].append(nn_p.astype(dt)); outs["m_p"].append(m_p.reshape(B, ML_HEADS).astype(dt))
        outs["ckv_s"].append(ckv[n_pt:m0].reshape(DB, S, KV_RANK))
        outs["kpe_s"].append(kpe[n_pt:m0].reshape(DB, S, QK_ROPE))
        outs["C_s"].append(C_s.astype(dt)); outs["n_s"].append(nn_s.astype(dt)); outs["m_s"].append(m_s.reshape(DB, ML_HEADS).astype(dt))

    y_prompt = h[:n_pt].reshape(B, seq, D)
    y_sample = h[n_pt:m0].reshape(DB, S, D)
    st = lambda k: jnp.stack(outs[k])
    return (y_prompt, y_sample, st("ckv_p"), st("kpe_p"), st("C_p"), st("n_p"), st("m_p"),
            st("ckv_s"), st("kpe_s"), st("C_s"), st("n_s"), st("m_s"))
```

```python
import functools
import math

import jax
import jax.numpy as jnp
from jax import lax
from jax.experimental import pallas as pl
from jax.experimental.pallas import tpu as pltpu

F32 = jnp.float32
BF16 = jnp.bfloat16

N_META = 16
MLA_HEADS = 8
QK_NOPE = 64
QK_ROPE = 32
V_DIM = 64
Q_RANK = 256
KV_RANK = 128
ROPE_THETA = 10000.0
SM_SCALE = (QK_NOPE + QK_ROPE) ** -0.5
ML_HEADS = 4
ML_DH = 128
ML_W = ML_HEADS * ML_DH
MLA_W = MLA_HEADS * V_DIM
PEER_HEADS = 8
N_KEYS = 128
PEER_DQ = 256
PEER_TOPK = 16
EPS = 1e-5

LANES = 128
SUBLANES = 8
ROW_BLOCK = 256
ATT_TQ = 128
ATT_TK = 256
ML_CHUNK = 128
PAGES_PER_STEP = 32
PEER_TB = 256
PEER_PIPE = 8
NEG_BIG = -1e30
VMEM_LIMIT = 48 * 1024 * 1024

ZW = 2560
Z_CQ, Z_CKV, Z_SMALL, Z_MQ, Z_MK, Z_MV, Z_MO = 0, 256, 384, 512, 1024, 1536, 2048


def _dot(a, b):
    return jnp.dot(a, b, preferred_element_type=F32)


def _dot_nt(a, b):
    return lax.dot_general(a, b, (((1,), (1,)), ((), ())), preferred_element_type=F32)


def _dot_tn(a, b):
    return lax.dot_general(a, b, (((0,), (0,)), ((), ())), preferred_element_type=F32)


def _split3(x):
    a = x.astype(BF16)
    r = x - a.astype(F32)
    b = r.astype(BF16)
    c = (r - b.astype(F32)).astype(BF16)
    return a, b, c


def _dot_exact_rhs01(x, sel):
    a, b, c = _split3(x)
    return _dot(a, sel) + _dot(b, sel) + _dot(c, sel)


def _imod(i, n):
    if n & (n - 1) == 0:
        return jnp.bitwise_and(i, n - 1)
    f = i.astype(F32)
    return (f - n * jnp.floor(f / n)).astype(jnp.int32)


def _layer_norm(x, g, b):
    mu = jnp.mean(x, axis=-1, keepdims=True)
    xc = x - mu
    var = jnp.mean(xc * xc, axis=-1, keepdims=True)
    return xc * lax.rsqrt(var + EPS) * g + b


def _rms_norm(x, g):
    return x * lax.rsqrt(jnp.mean(x * x, axis=-1, keepdims=True) + 1e-6) * g


def _cparams(sem, vmem=VMEM_LIMIT):
    return pltpu.CompilerParams(dimension_semantics=sem, vmem_limit_bytes=vmem)


def _full(shape):
    n = len(shape)
    return pl.BlockSpec(shape, lambda *_: (0,) * n)


def _proj_kernel(xm_ref, xs_ref, cos_ref, sin_ref, lng_ref, lnb_ref, win_ref, qng_ref, wqb_ref, wuk_ref, kvg_ref,
                 h_ref, ckv_ref, kpe_ref, gates_ref, qlat_ref, qrope_ref, mq_ref, mk_ref, mv_ref, og_ref,
                 *, apply_ln, nb_main):
    x = jnp.where(pl.program_id(0) < nb_main, xm_ref[...], xs_ref[...])
    h = _layer_norm(x, lng_ref[...], lnb_ref[...]) if apply_ln else x
    h_ref[...] = h
    z = _dot(h.astype(BF16), win_ref[...])
    ckv_ref[...] = _rms_norm(z[:, Z_CKV:Z_SMALL], kvg_ref[...])
    q = _dot(_rms_norm(z[:, Z_CQ:Z_CKV], qng_ref[...]).astype(BF16), wqb_ref[...])
    nope_w = MLA_HEADS * QK_NOPE
    qlat_ref[...] = _dot(q[:, :nope_w].astype(BF16), wuk_ref[...]).astype(BF16)
    c = cos_ref[...]
    s = sin_ref[...]
    x1 = q[:, nope_w:nope_w + LANES]
    x2 = q[:, nope_w + LANES:nope_w + 2 * LANES]
    qrope_ref[:, 0:LANES] = (x1 * c - x2 * s).astype(BF16)
    qrope_ref[:, LANES:2 * LANES] = (x1 * s + x2 * c).astype(BF16)
    zs = z[:, Z_SMALL:Z_MQ]
    half = QK_ROPE // 2
    lane = lax.broadcasted_iota(jnp.int32, zs.shape, 1)
    first = lane < half
    partner = jnp.where(first, pltpu.roll(zs, LANES - half, 1), pltpu.roll(zs, half, 1))
    kpe = zs * c + partner * jnp.where(first, -s, s)
    kpe_ref[...] = kpe[:, 0:QK_ROPE]
    gates_ref[...] = zs
    mq_ref[...] = z[:, Z_MQ:Z_MK].astype(BF16)
    mk_ref[...] = (z[:, Z_MK:Z_MV] * (ML_DH ** -0.5)).astype(BF16)
    mv_ref[...] = z[:, Z_MV:Z_MO].astype(BF16)
    og_ref[...] = jax.nn.sigmoid(z[:, Z_MO:ZW])


def _proj(x_main, x_small, cos_t, sin_t, lng, lnb, win, qng, wqb, wuk, kvg, *, apply_ln):
    n, d = cos_t.shape[0], x_main.shape[1]
    tm = ROW_BLOCK
    nb_main = n // tm - x_small.shape[0] // tm
    rows = lambda w: pl.BlockSpec((tm, w), lambda i: (i, 0))
    out_w = [(d, F32), (KV_RANK, F32), (QK_ROPE, F32), (LANES, F32), (MLA_HEADS * KV_RANK, BF16),
             (2 * LANES, BF16), (ML_W, BF16), (ML_W, BF16), (ML_W, BF16), (ML_W, F32)]
    return pl.pallas_call(
        functools.partial(_proj_kernel, apply_ln=apply_ln, nb_main=nb_main),
        grid=(n // tm,),
        in_specs=[pl.BlockSpec((tm, d), lambda i: (jnp.minimum(i, nb_main - 1), 0)),
                  pl.BlockSpec((tm, d), lambda i: (jnp.maximum(i - nb_main, 0), 0)), rows(LANES), rows(LANES),
                  _full(lng.shape), _full(lnb.shape), _full(win.shape), _full(qng.shape), _full(wqb.shape),
                  _full(wuk.shape), _full(kvg.shape)],
        out_specs=[rows(w) for w, _ in out_w],
        out_shape=[jax.ShapeDtypeStruct((n, w), dt) for w, dt in out_w],
        compiler_params=_cparams(("parallel",)),
        name="proj",
    )(x_main, x_small, cos_t, sin_t, lng, lnb, win, qng, wqb, wuk, kvg)


def _attn_prompt_kernel(q_ref, kt_ref, v_ref, o_ref, m_sc, acc_sc):
    i = pl.program_id(1)
    nh, tq, dk = q_ref.shape[1:]
    tk = ATT_TK
    reps = tk // LANES
    m_sc[...] = jnp.full(m_sc.shape, -jnp.inf, F32)
    acc_sc[...] = jnp.zeros(acc_sc.shape, F32)
    q_pos = i * tq + lax.broadcasted_iota(jnp.int32, (tq, tk), 0)
    col = lax.broadcasted_iota(jnp.int32, (tq, tk), 1)
    n_full = (i * tq + 1) // tk

    def chunk(c, masked):
        kt = kt_ref[0, c]
        v = v_ref[0, pl.ds(pl.multiple_of(c * tk, tk), tk), :]
        for hd in range(nh):
            rows = slice(hd * tq, (hd + 1) * tq)
            s = _dot(q_ref[0, hd], kt) * SM_SCALE
            if masked:
                s = jnp.where(col + c * tk <= q_pos, s, -jnp.inf)
            m_prev = m_sc[rows, :]
            m_new = jnp.maximum(m_prev, jnp.max(s, axis=1, keepdims=True))
            alpha = jnp.exp(m_prev - m_new)
            p = jnp.exp(s - jnp.tile(m_new, (1, reps)))
            acc_sc[rows, :] = jnp.tile(alpha, (1, 2)) * acc_sc[rows, :] + _dot(p.astype(BF16), v)
            m_sc[rows, :] = m_new

    def body(c, carry):
        chunk(c, False)
        return carry

    lax.fori_loop(0, n_full, body, 0)
    chunk(n_full, True)
    for hd in range(nh):
        acc = acc_sc[hd * tq:(hd + 1) * tq, :]
        o_ref[0, hd] = (acc[:, :KV_RANK] / acc[:, KV_RANK:]).astype(o_ref.dtype)


def _attn_prompt(q, kt, v):
    b, nh, tp, dk = q.shape
    tq = ATT_TQ
    return pl.pallas_call(
        _attn_prompt_kernel,
        grid=(b, tp // tq),
        in_specs=[pl.BlockSpec((1, nh, tq, dk), lambda bi, i: (bi, 0, i, 0)),
                  pl.BlockSpec((1,) + kt.shape[1:], lambda bi, i: (bi, 0, 0, 0)),
                  pl.BlockSpec((1,) + v.shape[1:], lambda bi, i: (bi, 0, 0))],
        out_specs=pl.BlockSpec((1, nh, tq, KV_RANK), lambda bi, i: (bi, 0, i, 0)),
        out_shape=jax.ShapeDtypeStruct((b, nh, tp, KV_RANK), BF16),
        scratch_shapes=[pltpu.VMEM((nh * tq, LANES), F32), pltpu.VMEM((nh * tq, 2 * KV_RANK), F32)],
        compiler_params=_cparams(("parallel", "arbitrary")),
        name="attn_prompt",
    )(q, kt, v)


def _attn_sample_kernel(pt_ref, q_ref, knew_ref, *refs, n_new, pps):
    ckv_refs, kpe_refs = refs[:pps], refs[pps:2 * pps]
    o_ref, m_sc, l_sc, acc_sc = refs[2 * pps:]
    g = pl.program_id(1)
    q = q_ref[0]
    ql, qr = q[:, :KV_RANK], q[:, KV_RANK:]

    @pl.when(g == 0)
    def _():
        m_sc[...] = jnp.full(m_sc.shape, -jnp.inf, F32)
        l_sc[...] = jnp.zeros(l_sc.shape, F32)
        acc_sc[...] = jnp.zeros(acc_sc.shape, F32)

    def update(s, vals):
        m_prev = m_sc[...]
        m_new = jnp.maximum(m_prev, jnp.max(s, axis=1, keepdims=True))
        alpha = jnp.exp(m_prev - m_new)
        p = jnp.exp(s - m_new)
        l_sc[...] = alpha * l_sc[...] + jnp.sum(p, axis=1, keepdims=True)
        acc = alpha * acc_sc[...]
        w = p.shape[1] // len(vals)
        for j, v in enumerate(vals):
            acc = acc + _dot(p[:, j * w:(j + 1) * w].astype(BF16), v)
        acc_sc[...] = acc
        m_sc[...] = m_new

    kc = [r[...].astype(BF16) for r in ckv_refs]
    s = jnp.concatenate([_dot_nt(ql, kc[j]) + _dot(qr, kpe_refs[j][...].astype(BF16)) for j in range(pps)],
                        axis=1) * SM_SCALE
    update(s, kc)

    @pl.when(g == pl.num_programs(1) - 1)
    def _():
        kn = knew_ref[0]
        s2 = _dot_nt(q, kn) * SM_SCALE
        row = lax.broadcasted_iota(jnp.int32, s2.shape, 0)
        colk = lax.broadcasted_iota(jnp.int32, s2.shape, 1)
        s2 = jnp.where((colk <= _imod(row, n_new)) & (colk < n_new), s2, -jnp.inf)
        update(s2, [kn[:, :KV_RANK]])
        o_ref[0] = (acc_sc[...] / l_sc[...]).astype(o_ref.dtype)


def _attn_sample(page_table, q, knew, cache_ckv, cache_kpe_t, n_new):
    db, r, dk = q.shape
    n_pages = page_table.shape[1]
    pps = math.gcd(n_pages, PAGES_PER_STEP)
    newp = knew.shape[1]

    def page_spec(arr, j):
        return pl.BlockSpec((None,) + arr.shape[1:], lambda b, g, pt: (pt[b, g * pps + j], 0, 0))

    grid_spec = pltpu.PrefetchScalarGridSpec(
        num_scalar_prefetch=1,
        grid=(db, n_pages // pps),
        in_specs=[pl.BlockSpec((1, r, dk), lambda b, g, pt: (b, 0, 0)),
                  pl.BlockSpec((1, newp, dk), lambda b, g, pt: (b, 0, 0))]
                 + [page_spec(cache_ckv, j) for j in range(pps)]
                 + [page_spec(cache_kpe_t, j) for j in range(pps)],
        out_specs=pl.BlockSpec((1, r, KV_RANK), lambda b, g, pt: (b, 0, 0)),
        scratch_shapes=[pltpu.VMEM((r, 1), F32), pltpu.VMEM((r, 1), F32), pltpu.VMEM((r, KV_RANK), F32)],
    )
    return pl.pallas_call(
        functools.partial(_attn_sample_kernel, n_new=n_new, pps=pps),
        grid_spec=grid_spec,
        out_shape=jax.ShapeDtypeStruct((db, r, KV_RANK), BF16),
        compiler_params=_cparams(("parallel", "arbitrary")),
        name="attn_sample",
    )(page_table, q, knew, *([cache_ckv] * pps), *([cache_kpe_t] * pps))


def _gate_prep_kernel(g_ref, bias_ref, valid_ref, triu_ref, o_ref):
    x = g_ref[...] + bias_ref[...]
    valid = valid_ref[...] > 0.5
    is_i = lax.broadcasted_iota(jnp.int32, x.shape, 1) < ML_HEADS
    lf = jnp.minimum(x, 0.0) - jnp.log1p(jnp.exp(-jnp.abs(x)))
    y = jnp.where(is_i, jnp.where(valid, x, NEG_BIG), jnp.where(valid, lf, 0.0))
    r, g8, l = x.shape
    cum = _dot_exact_rhs01(y.reshape(r * g8, l), triu_ref[...]).reshape(r, g8, l)
    o_ref[...] = jnp.where(is_i, y, cum)


def _gate_prep(gt, bias, valid):
    r, g8, l = gt.shape
    rb = 8 if r % 8 == 0 else r
    triu = (jnp.arange(l)[:, None] <= jnp.arange(l)[None, :]).astype(BF16)
    return pl.pallas_call(
        _gate_prep_kernel,
        grid=(r // rb,),
        in_specs=[pl.BlockSpec((rb, g8, l), lambda i: (i, 0, 0)), _full(bias.shape),
                  pl.BlockSpec((rb, 1, l), lambda i: (i, 0, 0)), _full(triu.shape)],
        out_specs=pl.BlockSpec((rb, g8, l), lambda i: (i, 0, 0)),
        out_shape=jax.ShapeDtypeStruct((r, g8, l), F32),
        compiler_params=_cparams(("parallel",)),
        name="gate_prep",
    )(gt, bias, valid, triu)


def _mlstm_kernel(q_ref, k_ref, v_ref, rf_ref, cf_ref, c0_ref, n0_ref, m0_ref,
                  h_ref, c_ref, n_ref, m_ref, c_sc, n_sc, m_sc):
    n_chunks = rf_ref.shape[1]
    lc = ML_CHUNK
    c_sc[...] = c0_ref[0]
    n_sc[...] = n0_ref[0]
    m_sc[...] = m0_ref[0]
    row = lax.broadcasted_iota(jnp.int32, (lc, lc), 0)
    colm = lax.broadcasted_iota(jnp.int32, (lc, lc), 1)
    causal = row >= colm

    def body(ci, carry):
        start = pl.multiple_of(ci * lc, lc)
        rf = rf_ref[0, ci]
        cf = cf_ref[0, ci]
        for hd in range(ML_HEADS):
            sl = slice(hd * ML_DH, (hd + 1) * ML_DH)
            q = q_ref[0, pl.ds(start, lc), sl]
            k = k_ref[0, pl.ds(start, lc), sl]
            v = v_ref[0, pl.ds(start, lc), sl]
            ig_row, b_row = rf[hd:hd + 1, :], rf[ML_HEADS + hd:ML_HEADS + hd + 1, :]
            ig_col, b_col = cf[:, hd:hd + 1], cf[:, ML_HEADS + hd:ML_HEADS + hd + 1]
            c0 = c_sc[hd]
            n0 = n_sc[hd:hd + 1, :]
            m0 = m_sc[0:1, hd:hd + 1]
            d = jnp.where(causal, b_col - b_row + ig_row, -jnp.inf)
            inter = b_col + m0
            m = jnp.maximum(inter, jnp.max(d, axis=1, keepdims=True))
            w = jnp.exp(d - m)
            a = jnp.exp(inter - m)
            wqk = w * _dot_nt(q, k)
            num = a * _dot_nt(q, c0.astype(BF16)) + _dot(wqk.astype(BF16), v)
            qf = q.astype(F32)
            den = a * jnp.sum(qf * n0, axis=1, keepdims=True) + jnp.sum(wqk, axis=1, keepdims=True)
            h_ref[0, pl.ds(start, lc), sl] = num / jnp.maximum(jnp.abs(den), jnp.exp(-m))
            b_last = b_col[lc - 1:lc, :]
            dec_row = b_last - b_row + ig_row
            m_last = jnp.maximum(b_last + m0, jnp.max(dec_row, axis=1, keepdims=True))
            ws_col = jnp.exp(b_last - b_col + ig_col - m_last)
            a_last = jnp.exp(b_last + m0 - m_last)
            c_sc[hd] = a_last * c0 + _dot_tn((v.astype(F32) * ws_col).astype(BF16), k)
            n_sc[hd:hd + 1, :] = a_last * n0 + jnp.sum(ws_col * k.astype(F32), axis=0, keepdims=True)
            m_sc[0:1, hd:hd + 1] = m_last
        return carry

    lax.fori_loop(0, n_chunks, body, 0)
    c_ref[0] = c_sc[...]
    n_ref[0] = n_sc[...]
    m_ref[0] = m_sc[...]


def _mlstm(q, k, v, rf, cf, c0, n0, m0):
    b, t, w = q.shape
    nc = rf.shape[1]
    seq = pl.BlockSpec((1, t, w), lambda i: (i, 0, 0))
    st = [pl.BlockSpec((1,) + c0.shape[1:], lambda i: (i, 0, 0, 0)),
          pl.BlockSpec((1,) + n0.shape[1:], lambda i: (i, 0, 0)),
          pl.BlockSpec((1,) + m0.shape[1:], lambda i: (i, 0, 0))]
    return pl.pallas_call(
        _mlstm_kernel,
        grid=(b,),
        in_specs=[seq, seq, seq,
                  pl.BlockSpec((1,) + rf.shape[1:], lambda i: (i, 0, 0, 0)),
                  pl.BlockSpec((1,) + cf.shape[1:], lambda i: (i, 0, 0, 0))] + st,
        out_specs=[seq] + st,
        out_shape=[jax.ShapeDtypeStruct((b, t, w), F32), jax.ShapeDtypeStruct(c0.shape, F32),
                   jax.ShapeDtypeStruct(n0.shape, F32), jax.ShapeDtypeStruct(m0.shape, F32)],
        scratch_shapes=[pltpu.VMEM(c0.shape[1:], F32), pltpu.VMEM(n0.shape[1:], F32), pltpu.VMEM(m0.shape[1:], F32)],
        compiler_params=_cparams(("parallel",)),
        name="mlstm",
    )(q, k, v, rf, cf, c0, n0, m0)


def _mix_kernel(olat_ref, hml_ref, og_ref, res_ref, wuv_ref, mhg_ref, wo_ref, g_ref, b_ref, o_ref, o8_ref, *, alpha):
    omla = _dot(olat_ref[...], wuv_ref[...])
    hml = hml_ref[...]
    parts = []
    for hd in range(ML_HEADS):
        x = hml[:, hd * ML_DH:(hd + 1) * ML_DH]
        mu = jnp.mean(x, axis=-1, keepdims=True)
        xc = x - mu
        var = jnp.mean(xc * xc, axis=-1, keepdims=True)
        parts.append(xc * lax.rsqrt(var + EPS))
    oml = (jnp.concatenate(parts, axis=1) * mhg_ref[...]) * og_ref[...]
    cat = jnp.concatenate([omla.astype(BF16), oml.astype(BF16)], axis=1)
    a = _dot(cat, wo_ref[...])
    h1 = _layer_norm(alpha * res_ref[...] + a, g_ref[...], b_ref[...])
    o_ref[...] = h1
    tm = h1.shape[0]
    for c in range(h1.shape[1] // LANES):
        o8_ref[pl.ds(c, tm, stride=SUBLANES), :] = h1[:, c * LANES:(c + 1) * LANES]


def _mix(olat, hml, og, res, wuv, mhg, wo, g, b, alpha):
    n, d = res.shape
    tm = ROW_BLOCK
    rows = lambda w: pl.BlockSpec((tm, w), lambda i: (i, 0))
    return pl.pallas_call(
        functools.partial(_mix_kernel, alpha=alpha),
        grid=(n // tm,),
        in_specs=[rows(olat.shape[1]), rows(ML_W), rows(ML_W), rows(d), _full(wuv.shape), _full(mhg.shape),
                  _full(wo.shape), _full(g.shape), _full(b.shape)],
        out_specs=[rows(d), pl.BlockSpec((tm * SUBLANES, LANES), lambda i: (i, 0))],
        out_shape=[jax.ShapeDtypeStruct((n, d), F32), jax.ShapeDtypeStruct((n * SUBLANES, LANES), F32)],
        compiler_params=_cparams(("parallel",)),
        name="mix",
    )(olat, hml, og, res, wuv, mhg, wo, g, b)


def _res_ln_kernel(res_ref, y8_ref, g_ref, b_ref, o_ref, *, alpha):
    tm, d = res_ref.shape
    y = jnp.concatenate([y8_ref[pl.ds(c, tm, stride=SUBLANES), :] for c in range(d // LANES)], axis=1)
    o_ref[...] = _layer_norm(alpha * res_ref[...] + y, g_ref[...], b_ref[...])


def _res_ln(res, y, g, b, alpha, r0, nr):
    d = res.shape[1]
    tm = ROW_BLOCK
    b0 = r0 // tm
    return pl.pallas_call(
        functools.partial(_res_ln_kernel, alpha=alpha),
        grid=(nr // tm,),
        in_specs=[pl.BlockSpec((tm, d), lambda i: (b0 + i, 0)),
                  pl.BlockSpec((tm * SUBLANES, LANES), lambda i: (b0 + i, 0)), _full(g.shape), _full(b.shape)],
        out_specs=pl.BlockSpec((tm, d), lambda i: (i, 0)),
        out_shape=jax.ShapeDtypeStruct((nr, d), F32),
        compiler_params=_cparams(("parallel",)),
        name="res_ln",
    )(res, y, g, b)


def _topk_rounds(jobs):
    def body(kk, carry):
        for s_sc, n_rows, tie, vals_sc, idx_sc in jobs:
            s = s_sc[0:n_rows, :]
            m = jnp.max(s, axis=0, keepdims=True)
            i = jnp.min(jnp.where(s == m, tie, jnp.inf), axis=0, keepdims=True)
            vals_sc[pl.ds(kk, 1), :] = m
            idx_sc[pl.ds(kk, 1), :] = i
            s_sc[0:n_rows, :] = jnp.where(tie == i, -jnp.inf, s)
        return carry

    lax.fori_loop(0, PEER_TOPK, body, 0)


def _candidate_blocks():
    k, sub = PEER_TOPK, SUBLANES
    need = [k // (i + 1) for i in range(k)]
    blocks, covered = [], set()
    for i in range(k):
        if need[i] >= sub:
            nj = -(-need[i] // sub) * sub
            blocks.append(("row", i, nj))
            covered |= {(i, j) for j in range(nj)}
    for i0 in range(0, k, sub):
        rest = [i for i in range(i0, i0 + sub) if need[i] < sub]
        if rest:
            for j in range(max(need[i] for i in rest)):
                blocks.append(("col", j, i0, rest[0]))
                covered |= {(i, j) for i in range(rest[0], i0 + sub)}
    assert all((i, j) in covered for i in range(k) for j in range(k) if (i + 1) * (j + 1) <= k)
    return blocks


CAND_BLOCKS = _candidate_blocks()
CAND_ROWS = sum(b[2] if b[0] == "row" else SUBLANES for b in CAND_BLOCKS)


def _route_kernel(x_ref, whi_ref, wlo_ref, khi_ref, klo_ref, idx_ref, g_ref,
                  q_sc, s0_sc, s1_sc, c_sc, sv0, si0, sv1, si1, cv, ci):
    hd = pl.program_id(1)
    kq = PEER_DQ // 2
    tm = x_ref.shape[0]

    @pl.when(hd == 0)
    def _():
        x = x_ref[...]
        xh = x.astype(BF16)
        xl = (x - xh.astype(F32)).astype(BF16)
        q = _dot(xh, whi_ref[...]) + _dot(xh, wlo_ref[...]) + _dot(xl, whi_ref[...])
        for j in range(PEER_HEADS):
            q_sc[j] = q[:, j * PEER_DQ:(j + 1) * PEER_DQ]

    qh_all = q_sc[hd]
    for p, s_sc in enumerate((s0_sc, s1_sc)):
        qq = qh_all[:, p * kq:(p + 1) * kq]
        qh = qq.astype(BF16)
        ql = (qq - qh.astype(F32)).astype(BF16)
        kh, kl = khi_ref[0, p], klo_ref[0, p]
        s_sc[...] = _dot_nt(kh, qh) + _dot_nt(kh, ql) + _dot_nt(kl, qh)
    key_iota = lax.broadcasted_iota(jnp.int32, (N_KEYS, tm), 0).astype(F32)
    _topk_rounds([(s0_sc, N_KEYS, key_iota, sv0, si0), (s1_sc, N_KEYS, key_iota, sv1, si1)])

    row_iota = lambda n: lax.broadcasted_iota(jnp.int32, (n, tm), 0).astype(F32)
    flat_parts, off = [], 0
    for blk in CAND_BLOCKS:
        if blk[0] == "row":
            _, i, nj = blk
            c_sc[off:off + nj, :] = sv0[i:i + 1, :] + sv1[0:nj, :]
            flat_parts.append(row_iota(nj) + float(i * PEER_TOPK))
            off += nj
        else:
            _, j, i0, imin = blk
            ii = row_iota(SUBLANES) + float(i0)
            c_sc[off:off + SUBLANES, :] = jnp.where(ii >= float(imin), sv0[i0:i0 + SUBLANES, :] + sv1[j:j + 1, :], -jnp.inf)
            flat_parts.append(ii * float(PEER_TOPK) + float(j))
            off += SUBLANES
    _topk_rounds([(c_sc, CAND_ROWS, jnp.concatenate(flat_parts, axis=0), cv, ci)])
    cidx = ci[...]
    i1 = jnp.floor(cidx * (1.0 / PEER_TOPK))
    i2 = cidx - i1 * PEER_TOPK
    e1 = jnp.zeros_like(cidx)
    e2 = jnp.zeros_like(cidx)
    for i in range(PEER_TOPK):
        e1 = jnp.where(i1 == float(i), si0[i:i + 1, :], e1)
        e2 = jnp.where(i2 == float(i), si1[i:i + 1, :], e2)
    idx_ref[0] = (e1 * float(N_KEYS) + e2).astype(jnp.int32)
    cvv = cv[...]
    e = jnp.exp(cvv - cvv[0:1, :])
    g_ref[0] = e / jnp.sum(e, axis=0, keepdims=True)


def _route(x, whi, wlo, khi, klo):
    n, d = x.shape
    tm = ROW_BLOCK
    out_spec = pl.BlockSpec((1, PEER_TOPK, tm), lambda i, h: (h, 0, i))
    key_spec = pl.BlockSpec((1, 2, N_KEYS, PEER_DQ // 2), lambda i, h: (h, 0, 0, 0))
    tk = pltpu.VMEM((PEER_TOPK, tm), F32)
    return pl.pallas_call(
        _route_kernel,
        grid=(n // tm, PEER_HEADS),
        in_specs=[pl.BlockSpec((tm, d), lambda i, h: (i, 0)), _full(whi.shape), _full(wlo.shape), key_spec, key_spec],
        out_specs=[out_spec, out_spec],
        out_shape=[jax.ShapeDtypeStruct((PEER_HEADS, PEER_TOPK, n), jnp.int32),
                   jax.ShapeDtypeStruct((PEER_HEADS, PEER_TOPK, n), F32)],
        scratch_shapes=[pltpu.VMEM((PEER_HEADS, tm, PEER_DQ), F32), pltpu.VMEM((N_KEYS, tm), F32),
                        pltpu.VMEM((N_KEYS, tm), F32), pltpu.VMEM((CAND_ROWS, tm), F32), tk, tk, tk, tk, tk, tk],
        compiler_params=_cparams(("parallel", "arbitrary")),
        name="peer_route",
    )(x, whi, wlo, khi, klo)


ROW_WORDS = 4
SLOTS = PEER_HEADS * PEER_TOPK


def _gather_rows(idx_ref, tab_ref, g_sc, t):
    grp = 8
    for g in range(SLOTS // grp):
        idx_g = idx_ref.at[pl.ds(t * SLOTS + g * grp, grp)]
        for j in range(grp):
            r = g * grp + j
            e4 = pl.multiple_of(idx_g[j], ROW_WORDS)
            g_sc[pl.ds(ROW_WORDS * r, ROW_WORDS), :] = tab_ref[pl.ds(e4, ROW_WORDS), :]


def _token_pipeline(tb, depth, gather, contract, finish):
    for b in range(depth):
        gather(b, b)

    def body(i, carry):
        t = depth * i
        res = [contract(t + b, b) for b in range(depth)]
        for b in range(depth):
            gather(jnp.minimum(t + depth + b, tb - 1), b)
        for b in range(depth):
            finish(t + b, res[b])
        return carry

    lax.fori_loop(0, tb // depth, body, 0)


def _diag_mask():
    rows16 = 2 * SUBLANES
    sub = lax.broadcasted_iota(jnp.int32, (rows16, SUBLANES * SLOTS), 0)
    lane = lax.broadcasted_iota(jnp.int32, (rows16, SUBLANES * SLOTS), 1)
    return _imod(lane, SUBLANES) == _imod(sub, SUBLANES), sub < SUBLANES


def _hi_lo_rows(x8):
    hi = x8.astype(BF16).astype(F32)
    return jnp.concatenate([hi, x8 - hi], axis=0).astype(BF16)


def _peer_u_kernel(idx_ref, x_ref, tab_ref, sel_ref, act_ref, *scratch):
    bufs, d_sc = scratch[:-1], scratch[-1]
    tb = x_ref.shape[0]
    diag, _ = _diag_mask()

    def gather(t, b):
        _gather_rows(idx_ref, tab_ref, bufs[b], t)

    def contract(t, b):
        gb = pltpu.bitcast(bufs[b][...], BF16)
        return _dot_nt(_hi_lo_rows(x_ref[t]), gb)

    def finish(t, d):
        d_sc[pl.ds(t, 1), :] = jnp.sum(jnp.where(diag, d, 0.0), axis=0, keepdims=True)

    _token_pipeline(tb, len(bufs), gather, contract, finish)
    act_ref[...] = _dot_exact_rhs01(d_sc[...], sel_ref[...])


def _peer_v_kernel(idx_ref, act_ref, gate_ref, tab_ref, selt_ref, o_ref, *scratch):
    bufs, w_sc = scratch[:-1], scratch[-1]
    tb = act_ref.shape[0]
    w = gate_ref[...] * jax.nn.gelu(act_ref[...], approximate=True)
    w_sc[...] = _dot_exact_rhs01(w, selt_ref[...])
    diag, top = _diag_mask()

    def gather(t, b):
        _gather_rows(idx_ref, tab_ref, bufs[b], t)

    def contract(t, b):
        gb = pltpu.bitcast(bufs[b][...], BF16)
        w16 = jnp.where(diag, jnp.broadcast_to(w_sc[pl.ds(t, 1), :], diag.shape), 0.0)
        hi = w16.astype(BF16).astype(F32)
        lhs = jnp.where(top, hi, w16 - hi).astype(BF16)
        return _dot(lhs, gb)

    def finish(t, o16):
        o_ref[pl.ds(pl.multiple_of(t * SUBLANES, SUBLANES), SUBLANES), :] = o16[0:SUBLANES] + o16[SUBLANES:]

    _token_pipeline(tb, len(bufs), gather, contract, finish)


def _table_spec(shape):
    return pl.BlockSpec(shape, lambda i: (0, 0), pipeline_mode=pl.Buffered(1))


def _peer_scratch(tb):
    gbuf = pltpu.VMEM((ROW_WORDS * SLOTS, LANES), jnp.uint32)
    return [gbuf] * PEER_PIPE + [pltpu.VMEM((tb, SUBLANES * SLOTS), F32)]


def _peer_u(idx4, x8, tab, sel):
    n = x8.shape[0]
    tb = PEER_TB
    return pl.pallas_call(
        _peer_u_kernel,
        grid=(n // tb,),
        in_specs=[pl.BlockSpec((tb * SLOTS,), lambda i: (i,), memory_space=pltpu.SMEM),
                  pl.BlockSpec((tb, SUBLANES, LANES), lambda i: (i, 0, 0)),
                  _table_spec(tab.shape), _full(sel.shape)],
        out_specs=pl.BlockSpec((tb, SLOTS), lambda i: (i, 0)),
        out_shape=jax.ShapeDtypeStruct((n, SLOTS), F32),
        scratch_shapes=_peer_scratch(tb),
        compiler_params=_cparams(("parallel",), vmem=56 * 1024 * 1024),
        name="peer_u",
    )(idx4, x8, tab, sel)


def _peer_v(idx4, act, gate, tab, selt):
    n = act.shape[0]
    tb = PEER_TB
    return pl.pallas_call(
        _peer_v_kernel,
        grid=(n // tb,),
        in_specs=[pl.BlockSpec((tb * SLOTS,), lambda i: (i,), memory_space=pltpu.SMEM),
                  pl.BlockSpec((tb, SLOTS), lambda i: (i, 0)), pl.BlockSpec((tb, SLOTS), lambda i: (i, 0)),
                  _table_spec(tab.shape), _full(selt.shape)],
        out_specs=pl.BlockSpec((tb * SUBLANES, LANES), lambda i: (i, 0)),
        out_shape=jax.ShapeDtypeStruct((n * SUBLANES, LANES), F32),
        scratch_shapes=_peer_scratch(tb),
        compiler_params=_cparams(("parallel",), vmem=56 * 1024 * 1024),
        name="peer_v",
    )(idx4, act, gate, tab, selt)


def _pack_kernel(t_ref, o_ref, tile_sc):
    r, d = t_ref.shape
    x = t_ref[...]
    for c in range(d // LANES):
        tile_sc[pl.ds(c, r, stride=SUBLANES), :] = x[:, c * LANES:(c + 1) * LANES]
    o_ref[...] = pltpu.bitcast(tile_sc[...].astype(BF16), jnp.uint32)


def _pack_table(t):
    e, d = t.shape
    rb = ROW_BLOCK
    return pl.pallas_call(
        _pack_kernel,
        grid=(e // rb,),
        in_specs=[pl.BlockSpec((rb, d), lambda i: (i, 0))],
        out_specs=pl.BlockSpec((rb * ROW_WORDS, LANES), lambda i: (i, 0)),
        out_shape=jax.ShapeDtypeStruct((e * ROW_WORDS, LANES), jnp.uint32),
        scratch_shapes=[pltpu.VMEM((rb * d // LANES, LANES), F32)],
        compiler_params=_cparams(("parallel",)),
        name="pack_table",
    )(t)


def _peer(x, x8, whi, wlo, khi, klo, tab_u, tab_v):
    n, d = x.shape
    idx_t, gate_t = _route(x, whi, wlo, khi, klo)
    idx4 = (idx_t.reshape(SLOTS, n).T * ROW_WORDS).reshape(n * SLOTS)
    gate = gate_t.reshape(SLOTS, n).T
    grp = jnp.arange(SUBLANES * SLOTS) // SUBLANES
    sel = (grp[:, None] == jnp.arange(SLOTS)[None, :]).astype(BF16)
    act = _peer_u(idx4, x8.reshape(n, SUBLANES, LANES), tab_u, sel)
    return _peer_v(idx4, act, gate, tab_v, sel.T)


def _pad_rows(x, n):
    return jnp.pad(x, ((0, n - x.shape[0]),) + ((0, 0),) * (x.ndim - 1))


def _rope_tables(pos):
    half = QK_ROPE // 2
    inv = ROPE_THETA ** (-jnp.arange(half, dtype=F32) / half)
    ang = pos.astype(F32)[:, None] * inv
    reps = LANES // half
    return jnp.tile(jnp.cos(ang), (1, reps)), jnp.tile(jnp.sin(ang), (1, reps))


def _layer_weights(w_in, w_qb, w_kvb):
    d = w_in.shape[0]
    o_ckv = Q_RANK
    o_kr = o_ckv + KV_RANK
    o_mq = o_kr + QK_ROPE
    o_g = o_mq + 4 * ML_W
    small = jnp.concatenate([w_in[:, o_kr:o_mq], w_in[:, o_g:o_g + 2 * ML_HEADS],
                             jnp.zeros((d, LANES - QK_ROPE - 2 * ML_HEADS), w_in.dtype)], axis=1)
    win = jnp.concatenate([w_in[:, :o_kr], small, w_in[:, o_mq:o_g]], axis=1).astype(BF16)
    half = QK_ROPE // 2
    wq = w_qb.reshape(Q_RANK, MLA_HEADS, QK_NOPE + QK_ROPE)
    wqb = jnp.concatenate([wq[..., :QK_NOPE].reshape(Q_RANK, -1), wq[..., QK_NOPE:QK_NOPE + half].reshape(Q_RANK, -1),
                           wq[..., QK_NOPE + half:].reshape(Q_RANK, -1)], axis=1).astype(BF16)
    wkv = w_kvb.reshape(KV_RANK, MLA_HEADS, QK_NOPE + V_DIM)
    eye = jnp.eye(MLA_HEADS, dtype=w_kvb.dtype)
    wuk = jnp.einsum('chd,hg->hdgc', wkv[..., :QK_NOPE], eye).reshape(MLA_HEADS * QK_NOPE, MLA_HEADS * KV_RANK)
    wuv = jnp.einsum('chv,hg->hcgv', wkv[..., QK_NOPE:], eye).reshape(MLA_HEADS * KV_RANK, MLA_W)
    return win, wqb, wuk.astype(BF16), wuv.astype(BF16)


def _hi_lo(w):
    hi = w.astype(BF16)
    return hi, (w - hi.astype(F32)).astype(BF16)


def kernel(x_prompt, x_sample, cache_ckv, cache_kpe, page_table, state_C, state_n, state_m, meta_tokens,
           ln_in_g, ln_in_b, w_in, b_gates, q_norm_g, w_qb, kv_norm_g, w_kvb, mh_norm_g, w_o, ln1_g, ln1_b,
           w_pq, sub_keys, expert_u, expert_v, ln2_g, ln2_b):
    B, seq, D = x_prompt.shape
    DB, S = x_sample.shape[:2]
    depth = w_in.shape[0]
    dt = x_prompt.dtype
    T = seq + N_META
    page = cache_ckv.shape[2]
    past_len = page_table.shape[1] * page
    alpha = (2.0 * depth) ** 0.25
    row = lambda a: a.reshape(1, -1)

    n_pt, n_s = B * seq, DB * S
    assert n_pt % ROW_BLOCK == 0, "prompt rows must fill whole row blocks"
    m0 = n_pt + n_s
    n_small = -(-(n_s + N_META) // ROW_BLOCK) * ROW_BLOCK
    n_pad = n_pt + n_small
    x_small = _pad_rows(jnp.concatenate([x_sample.reshape(n_s, D), meta_tokens.astype(dt)], axis=0), n_small)
    pos = jnp.concatenate([jnp.tile(N_META + jnp.arange(seq), B), jnp.tile(past_len + jnp.arange(S), DB),
                           jnp.arange(N_META), jnp.zeros((n_pad - m0 - N_META,), jnp.int32)])
    cos_t, sin_t = _rope_tables(pos)

    t_att = -(-T // ATT_TK) * ATT_TK
    nc_p = -(-T // ML_CHUNK)
    t_ml = nc_p * ML_CHUNK
    new_pad = 16

    def prompt_seqs(a, t_total):
        w = a.shape[-1]
        parts = [jnp.broadcast_to(a[m0:m0 + N_META][None], (B, N_META, w)), a[:n_pt].reshape(B, seq, w)]
        if t_total > T:
            parts.append(jnp.zeros((B, t_total - T, w), a.dtype))
        return jnp.concatenate(parts, axis=1)

    def sample_seqs(a, t_total):
        return jnp.pad(a[n_pt:m0].reshape(DB, S, a.shape[-1]), ((0, 0), (0, t_total - S), (0, 0)))

    def flat_rows(prompt_part, sample_part, meta_part):
        w = meta_part.shape[-1]
        return _pad_rows(jnp.concatenate([prompt_part.reshape(n_pt, w), sample_part.reshape(n_s, w), meta_part], axis=0),
                         n_pad)

    def gate_forms(g_seq, n_valid):
        nb, t_total = g_seq.shape[:2]
        nc = t_total // ML_CHUNK
        gp = g_seq.reshape(nb * nc, ML_CHUNK, 2 * ML_HEADS).transpose(0, 2, 1)
        valid = jnp.broadcast_to((jnp.arange(t_total) < n_valid).astype(F32)[None], (nb, t_total)).reshape(nb * nc, 1, ML_CHUNK)
        rf = _gate_prep(gp, bias, valid).reshape(nb, nc, 2 * ML_HEADS, ML_CHUNK)
        return rf, rf.transpose(0, 1, 3, 2)

    outs = {k: [] for k in ("ckv_p", "kpe_p", "C_p", "n_p", "m_p", "ckv_s", "kpe_s", "C_s", "n_s", "m_s")}
    h_main, h_small = x_prompt.reshape(n_pt, D), x_small
    for l in range(depth):
        win, wqb, wuk, wuv = _layer_weights(w_in[l], w_qb[l], w_kvb[l])
        (h, ckv, kpe, zs, qlat, qrope, mq, mk, mv, og) = _proj(
            h_main, h_small, cos_t, sin_t, row(ln_in_g), row(ln_in_b), win, row(q_norm_g[l]), wqb, wuk,
            row(kv_norm_g[l]), apply_ln=(l == 0))
        gates = zs[:, QK_ROPE:QK_ROPE + 2 * ML_HEADS]
        bias = b_gates[l].astype(F32).reshape(1, 2 * ML_HEADS, 1)

        half = QK_ROPE // 2
        dk = KV_RANK + QK_ROPE
        qcat = jnp.concatenate([qlat.reshape(n_pad, MLA_HEADS, KV_RANK),
                                qrope[:, :LANES].reshape(n_pad, MLA_HEADS, half),
                                qrope[:, LANES:].reshape(n_pad, MLA_HEADS, half)], axis=-1)
        kcat = jnp.concatenate([ckv, kpe], axis=-1).astype(BF16)

        q_p = prompt_seqs(qcat.reshape(n_pad, MLA_HEADS * dk), t_att).reshape(B, t_att, MLA_HEADS, dk)
        k_p = prompt_seqs(kcat, t_att)
        kt_p = k_p.reshape(B, t_att // ATT_TK, ATT_TK, dk).transpose(0, 1, 3, 2)
        v_p = jnp.concatenate([k_p[..., :KV_RANK], jnp.ones((B, t_att, KV_RANK), BF16)], axis=-1)
        o_p = _attn_prompt(q_p.transpose(0, 2, 1, 3), kt_p, v_p)

        q_s = qcat[n_pt:m0].reshape(DB, S, MLA_HEADS, dk).transpose(0, 2, 1, 3).reshape(DB, MLA_HEADS * S, dk)
        o_s = _attn_sample(page_table, q_s, sample_seqs(kcat, new_pad), cache_ckv[l],
                           jnp.swapaxes(cache_kpe[l], -1, -2), S)
        olat = flat_rows(o_p[:, :, N_META:T].transpose(0, 2, 1, 3),
                         o_s.reshape(DB, MLA_HEADS, S, KV_RANK).transpose(0, 2, 1, 3),
                         o_p[0, :, :N_META].transpose(1, 0, 2).reshape(N_META, MLA_HEADS * KV_RANK))

        rf_p, cf_p = gate_forms(prompt_seqs(gates, t_ml), T)
        h_p, C_p, nn_p, m_p = _mlstm(prompt_seqs(mq, t_ml), prompt_seqs(mk, t_ml), prompt_seqs(mv, t_ml), rf_p, cf_p,
                                     jnp.zeros((B, ML_HEADS, ML_DH, ML_DH), F32), jnp.zeros((B, ML_HEADS, ML_DH), F32),
                                     jnp.zeros((B, 1, ML_HEADS), F32))
        rf_s, cf_s = gate_forms(sample_seqs(gates, ML_CHUNK), S)
        h_s, C_s, nn_s, m_s = _mlstm(sample_seqs(mq, ML_CHUNK), sample_seqs(mk, ML_CHUNK), sample_seqs(mv, ML_CHUNK),
                                     rf_s, cf_s, state_C[l].astype(F32), state_n[l].astype(F32),
                                     state_m[l].astype(F32).reshape(DB, 1, ML_HEADS))
        hml = flat_rows(h_p[:, N_META:T], h_s[:, :S], h_p[0, :N_META])

        h1, h1_tiles = _mix(olat, hml, og, h, wuv, row(mh_norm_g[l]), w_o[l].astype(BF16), row(ln1_g[l]), row(ln1_b[l]), alpha)
        whi, wlo = _hi_lo(w_pq[l])
        khi, klo = _hi_lo(sub_keys[l])
        y = _peer(h1, h1_tiles, whi, wlo, khi, klo, _pack_table(expert_u[l]), _pack_table(expert_v[l]))
        h_main = _res_ln(h1, y, row(ln2_g[l]), row(ln2_b[l]), alpha, 0, n_pt)
        h_small = _res_ln(h1, y, row(ln2_g[l]), row(ln2_b[l]), alpha, n_pt, n_small)

        outs["ckv_p"].append(prompt_seqs(ckv, T))
        outs["kpe_p"].append(prompt_seqs(kpe, T))
        outs["C_p"].append(C_p.astype(dt)); outs["n_p"].append(nn_p.astype(dt)); outs["m_p"].append(m_p.reshape(B, ML_HEADS).astype(dt))
        outs["ckv_s"].append(ckv[n_pt:m0].reshape(DB, S, KV_RANK))
        outs["kpe_s"].append(kpe[n_pt:m0].reshape(DB, S, QK_ROPE))
        outs["C_s"].append(C_s.astype(dt)); outs["n_s"].append(nn_s.astype(dt)); outs["m_s"].append(m_s.reshape(DB, ML_HEADS).astype(dt))

    y_prompt = h_main.reshape(B, seq, D)
    y_sample = h_small[:n_s].reshape(DB, S, D)
    st = lambda k: jnp.stack(outs[k])
    return (y_prompt, y_sample, st("ckv_p"), st("kpe_p"), st("C_p"), st("n_p"), st("m_p"),
            st("ckv_s"), st("kpe_s"), st("C_s"), st("n_s"), st("m_s"))
```

```python
import functools
import math

import jax
import jax.numpy as jnp
from jax import lax
from jax.experimental import pallas as pl
from jax.experimental.pallas import tpu as pltpu

F32 = jnp.float32
BF16 = jnp.bfloat16

N_META = 16
MLA_HEADS = 8
QK_NOPE = 64
QK_ROPE = 32
V_DIM = 64
Q_RANK = 256
KV_RANK = 128
ROPE_THETA = 10000.0
SM_SCALE = (QK_NOPE + QK_ROPE) ** -0.5
ML_HEADS = 4
ML_DH = 128
ML_W = ML_HEADS * ML_DH
MLA_W = MLA_HEADS * V_DIM
PEER_HEADS = 8
N_KEYS = 128
PEER_DQ = 256
PEER_TOPK = 16
EPS = 1e-5

LANES = 128
SUBLANES = 8
ROW_BLOCK = 256
ATT_TQ = 256
ATT_TK = 256
ML_CHUNK = 128
PAGES_PER_STEP = 32
PEER_TB = 256
PEER_PIPE = 8
NEG_BIG = -1e30
VMEM_LIMIT = 48 * 1024 * 1024

ZW = 2560
Z_CQ, Z_CKV, Z_SMALL, Z_MQ, Z_MK, Z_MV, Z_MO = 0, 256, 384, 512, 1024, 1536, 2048


def _dot(a, b):
    return jnp.dot(a, b, preferred_element_type=F32)


def _dot_nt(a, b):
    return lax.dot_general(a, b, (((1,), (1,)), ((), ())), preferred_element_type=F32)


def _dot_tn(a, b):
    return lax.dot_general(a, b, (((0,), (0,)), ((), ())), preferred_element_type=F32)


def _split3(x):
    a = x.astype(BF16)
    r = x - a.astype(F32)
    b = r.astype(BF16)
    c = (r - b.astype(F32)).astype(BF16)
    return a, b, c


def _dot_exact_rhs01(x, sel):
    a, b, c = _split3(x)
    return _dot(a, sel) + _dot(b, sel) + _dot(c, sel)


def _imod(i, n):
    if n & (n - 1) == 0:
        return jnp.bitwise_and(i, n - 1)
    f = i.astype(F32)
    return (f - n * jnp.floor(f / n)).astype(jnp.int32)


def _layer_norm(x, g, b):
    mu = jnp.mean(x, axis=-1, keepdims=True)
    xc = x - mu
    var = jnp.mean(xc * xc, axis=-1, keepdims=True)
    return xc * lax.rsqrt(var + EPS) * g + b


def _rms_norm(x, g):
    return x * lax.rsqrt(jnp.mean(x * x, axis=-1, keepdims=True) + 1e-6) * g


def _cparams(sem, vmem=VMEM_LIMIT):
    return pltpu.CompilerParams(dimension_semantics=sem, vmem_limit_bytes=vmem)


def _full(shape):
    n = len(shape)
    return pl.BlockSpec(shape, lambda *_: (0,) * n)


def _proj_kernel(xm_ref, xs_ref, cos_ref, sin_ref, lng_ref, lnb_ref, win_ref, qng_ref, wqb_ref, wuk_ref, kvg_ref,
                 h_ref, ckv_ref, kpe_ref, gates_ref, qlat_ref, qrope_ref, mq_ref, mk_ref, mv_ref, og_ref,
                 *, apply_ln, nb_main):
    x = jnp.where(pl.program_id(0) < nb_main, xm_ref[...], xs_ref[...])
    h = _layer_norm(x, lng_ref[...], lnb_ref[...]) if apply_ln else x
    h_ref[...] = h
    z = _dot(h.astype(BF16), win_ref[...])
    ckv_ref[...] = _rms_norm(z[:, Z_CKV:Z_SMALL], kvg_ref[...])
    q = _dot(_rms_norm(z[:, Z_CQ:Z_CKV], qng_ref[...]).astype(BF16), wqb_ref[...])
    nope_w = MLA_HEADS * QK_NOPE
    qlat_ref[...] = _dot(q[:, :nope_w].astype(BF16), wuk_ref[...]).astype(BF16)
    c = cos_ref[...]
    s = sin_ref[...]
    x1 = q[:, nope_w:nope_w + LANES]
    x2 = q[:, nope_w + LANES:nope_w + 2 * LANES]
    qrope_ref[:, 0:LANES] = (x1 * c - x2 * s).astype(BF16)
    qrope_ref[:, LANES:2 * LANES] = (x1 * s + x2 * c).astype(BF16)
    zs = z[:, Z_SMALL:Z_MQ]
    half = QK_ROPE // 2
    lane = lax.broadcasted_iota(jnp.int32, zs.shape, 1)
    first = lane < half
    partner = jnp.where(first, pltpu.roll(zs, LANES - half, 1), pltpu.roll(zs, half, 1))
    kpe = zs * c + partner * jnp.where(first, -s, s)
    kpe_ref[...] = kpe[:, 0:QK_ROPE]
    gates_ref[...] = zs
    mq_ref[...] = z[:, Z_MQ:Z_MK].astype(BF16)
    mk_ref[...] = (z[:, Z_MK:Z_MV] * (ML_DH ** -0.5)).astype(BF16)
    mv_ref[...] = z[:, Z_MV:Z_MO].astype(BF16)
    og_ref[...] = jax.nn.sigmoid(z[:, Z_MO:ZW])


def _proj(x_main, x_small, cos_t, sin_t, lng, lnb, win, qng, wqb, wuk, kvg, *, apply_ln):
    n, d = cos_t.shape[0], x_main.shape[1]
    tm = ROW_BLOCK
    nb_main = n // tm - x_small.shape[0] // tm
    rows = lambda w: pl.BlockSpec((tm, w), lambda i: (i, 0))
    out_w = [(d, F32), (KV_RANK, F32), (QK_ROPE, F32), (LANES, F32), (MLA_HEADS * KV_RANK, BF16),
             (2 * LANES, BF16), (ML_W, BF16), (ML_W, BF16), (ML_W, BF16), (ML_W, F32)]
    return pl.pallas_call(
        functools.partial(_proj_kernel, apply_ln=apply_ln, nb_main=nb_main),
        grid=(n // tm,),
        in_specs=[pl.BlockSpec((tm, d), lambda i: (jnp.minimum(i, nb_main - 1), 0)),
                  pl.BlockSpec((tm, d), lambda i: (jnp.maximum(i - nb_main, 0), 0)), rows(LANES), rows(LANES),
                  _full(lng.shape), _full(lnb.shape), _full(win.shape), _full(qng.shape), _full(wqb.shape),
                  _full(wuk.shape), _full(kvg.shape)],
        out_specs=[rows(w) for w, _ in out_w],
        out_shape=[jax.ShapeDtypeStruct((n, w), dt) for w, dt in out_w],
        compiler_params=_cparams(("parallel",)),
        name="proj",
    )(x_main, x_small, cos_t, sin_t, lng, lnb, win, qng, wqb, wuk, kvg)


def _attn_prompt_kernel(q_ref, kt_ref, v_ref, o_ref, m_sc, acc_sc):
    i = pl.program_id(1)
    nh, tq, dk = q_ref.shape[1:]
    tk = ATT_TK
    reps = tk // LANES
    m_sc[...] = jnp.full(m_sc.shape, -jnp.inf, F32)
    acc_sc[...] = jnp.zeros(acc_sc.shape, F32)
    q_pos = i * tq + lax.broadcasted_iota(jnp.int32, (tq, tk), 0)
    col = lax.broadcasted_iota(jnp.int32, (tq, tk), 1)
    n_full = (i * tq + 1) // tk

    def chunk(c, masked):
        kt = kt_ref[0, c]
        v = v_ref[0, pl.ds(pl.multiple_of(c * tk, tk), tk), :]
        for hd in range(nh):
            rows = slice(hd * tq, (hd + 1) * tq)
            s = _dot(q_ref[0, hd], kt) * SM_SCALE
            if masked:
                s = jnp.where(col + c * tk <= q_pos, s, -jnp.inf)
            m_prev = m_sc[rows, :]
            m_new = jnp.maximum(m_prev, jnp.max(s, axis=1, keepdims=True))
            alpha = jnp.exp(m_prev - m_new)
            p = jnp.exp(s - jnp.tile(m_new, (1, reps)))
            acc_sc[rows, :] = jnp.tile(alpha, (1, 2)) * acc_sc[rows, :] + _dot(p.astype(BF16), v)
            m_sc[rows, :] = m_new

    def body(c, carry):
        chunk(c, False)
        return carry

    lax.fori_loop(0, n_full, body, 0)
    chunk(n_full, True)
    for hd in range(nh):
        acc = acc_sc[hd * tq:(hd + 1) * tq, :]
        o_ref[0, hd] = (acc[:, :KV_RANK] / acc[:, KV_RANK:]).astype(o_ref.dtype)


def _attn_prompt(q, kt, v):
    b, nh, tp, dk = q.shape
    tq = ATT_TQ
    return pl.pallas_call(
        _attn_prompt_kernel,
        grid=(b, tp // tq),
        in_specs=[pl.BlockSpec((1, nh, tq, dk), lambda bi, i: (bi, 0, i, 0)),
                  pl.BlockSpec((1,) + kt.shape[1:], lambda bi, i: (bi, 0, 0, 0)),
                  pl.BlockSpec((1,) + v.shape[1:], lambda bi, i: (bi, 0, 0))],
        out_specs=pl.BlockSpec((1, nh, tq, KV_RANK), lambda bi, i: (bi, 0, i, 0)),
        out_shape=jax.ShapeDtypeStruct((b, nh, tp, KV_RANK), BF16),
        scratch_shapes=[pltpu.VMEM((nh * tq, LANES), F32), pltpu.VMEM((nh * tq, 2 * KV_RANK), F32)],
        compiler_params=_cparams(("parallel", "arbitrary")),
        name="attn_prompt",
    )(q, kt, v)


def _attn_sample_kernel(pt_ref, q_ref, knew_ref, *refs, n_new, pps):
    ckv_refs, kpe_refs = refs[:pps], refs[pps:2 * pps]
    o_ref, m_sc, l_sc, acc_sc = refs[2 * pps:]
    g = pl.program_id(1)
    q = q_ref[0]
    ql, qr = q[:, :KV_RANK], q[:, KV_RANK:]

    @pl.when(g == 0)
    def _():
        m_sc[...] = jnp.full(m_sc.shape, -jnp.inf, F32)
        l_sc[...] = jnp.zeros(l_sc.shape, F32)
        acc_sc[...] = jnp.zeros(acc_sc.shape, F32)

    def update(s, vals):
        m_prev = m_sc[...]
        m_new = jnp.maximum(m_prev, jnp.max(s, axis=1, keepdims=True))
        alpha = jnp.exp(m_prev - m_new)
        p = jnp.exp(s - m_new)
        l_sc[...] = alpha * l_sc[...] + jnp.sum(p, axis=1, keepdims=True)
        acc = alpha * acc_sc[...]
        w = p.shape[1] // len(vals)
        for j, v in enumerate(vals):
            acc = acc + _dot(p[:, j * w:(j + 1) * w].astype(BF16), v)
        acc_sc[...] = acc
        m_sc[...] = m_new

    kc = [r[...].astype(BF16) for r in ckv_refs]
    s = jnp.concatenate([_dot_nt(ql, kc[j]) + _dot(qr, kpe_refs[j][...].astype(BF16)) for j in range(pps)],
                        axis=1) * SM_SCALE
    update(s, kc)

    @pl.when(g == pl.num_programs(1) - 1)
    def _():
        kn = knew_ref[0]
        s2 = _dot_nt(q, kn) * SM_SCALE
        row = lax.broadcasted_iota(jnp.int32, s2.shape, 0)
        colk = lax.broadcasted_iota(jnp.int32, s2.shape, 1)
        s2 = jnp.where((colk <= _imod(row, n_new)) & (colk < n_new), s2, -jnp.inf)
        update(s2, [kn[:, :KV_RANK]])
        o_ref[0] = (acc_sc[...] / l_sc[...]).astype(o_ref.dtype)


def _attn_sample(page_table, q, knew, cache_ckv, cache_kpe_t, n_new):
    db, r, dk = q.shape
    n_pages = page_table.shape[1]
    pps = math.gcd(n_pages, PAGES_PER_STEP)
    newp = knew.shape[1]

    def page_spec(arr, j):
        return pl.BlockSpec((None,) + arr.shape[1:], lambda b, g, pt: (pt[b, g * pps + j], 0, 0))

    grid_spec = pltpu.PrefetchScalarGridSpec(
        num_scalar_prefetch=1,
        grid=(db, n_pages // pps),
        in_specs=[pl.BlockSpec((1, r, dk), lambda b, g, pt: (b, 0, 0)),
                  pl.BlockSpec((1, newp, dk), lambda b, g, pt: (b, 0, 0))]
                 + [page_spec(cache_ckv, j) for j in range(pps)]
                 + [page_spec(cache_kpe_t, j) for j in range(pps)],
        out_specs=pl.BlockSpec((1, r, KV_RANK), lambda b, g, pt: (b, 0, 0)),
        scratch_shapes=[pltpu.VMEM((r, 1), F32), pltpu.VMEM((r, 1), F32), pltpu.VMEM((r, KV_RANK), F32)],
    )
    return pl.pallas_call(
        functools.partial(_attn_sample_kernel, n_new=n_new, pps=pps),
        grid_spec=grid_spec,
        out_shape=jax.ShapeDtypeStruct((db, r, KV_RANK), BF16),
        compiler_params=_cparams(("parallel", "arbitrary")),
        name="attn_sample",
    )(page_table, q, knew, *([cache_ckv] * pps), *([cache_kpe_t] * pps))


def _gate_prep_kernel(g_ref, bias_ref, valid_ref, triu_ref, o_ref):
    x = g_ref[...] + bias_ref[...]
    valid = valid_ref[...] > 0.5
    is_i = lax.broadcasted_iota(jnp.int32, x.shape, 1) < ML_HEADS
    lf = jnp.minimum(x, 0.0) - jnp.log1p(jnp.exp(-jnp.abs(x)))
    y = jnp.where(is_i, jnp.where(valid, x, NEG_BIG), jnp.where(valid, lf, 0.0))
    r, g8, l = x.shape
    cum = _dot_exact_rhs01(y.reshape(r * g8, l), triu_ref[...]).reshape(r, g8, l)
    o_ref[...] = jnp.where(is_i, y, cum)


def _gate_prep(gt, bias, valid):
    r, g8, l = gt.shape
    rb = 8 if r % 8 == 0 else r
    triu = (jnp.arange(l)[:, None] <= jnp.arange(l)[None, :]).astype(BF16)
    return pl.pallas_call(
        _gate_prep_kernel,
        grid=(r // rb,),
        in_specs=[pl.BlockSpec((rb, g8, l), lambda i: (i, 0, 0)), _full(bias.shape),
                  pl.BlockSpec((rb, 1, l), lambda i: (i, 0, 0)), _full(triu.shape)],
        out_specs=pl.BlockSpec((rb, g8, l), lambda i: (i, 0, 0)),
        out_shape=jax.ShapeDtypeStruct((r, g8, l), F32),
        compiler_params=_cparams(("parallel",)),
        name="gate_prep",
    )(gt, bias, valid, triu)


def _mlstm_kernel(q_ref, k_ref, v_ref, rf_ref, cf_ref, c0_ref, n0_ref, m0_ref,
                  h_ref, c_ref, n_ref, m_ref, c_sc, n_sc, m_sc):
    n_chunks = rf_ref.shape[1]
    lc = ML_CHUNK
    c_sc[...] = c0_ref[0]
    n_sc[...] = n0_ref[0]
    m_sc[...] = m0_ref[0]
    row = lax.broadcasted_iota(jnp.int32, (lc, lc), 0)
    colm = lax.broadcasted_iota(jnp.int32, (lc, lc), 1)
    causal = row >= colm

    def body(ci, carry):
        start = pl.multiple_of(ci * lc, lc)
        rf = rf_ref[0, ci]
        cf = cf_ref[0, ci]
        for hd in range(ML_HEADS):
            sl = slice(hd * ML_DH, (hd + 1) * ML_DH)
            q = q_ref[0, pl.ds(start, lc), sl]
            k = k_ref[0, pl.ds(start, lc), sl]
            v = v_ref[0, pl.ds(start, lc), sl]
            ig_row, b_row = rf[hd:hd + 1, :], rf[ML_HEADS + hd:ML_HEADS + hd + 1, :]
            ig_col, b_col = cf[:, hd:hd + 1], cf[:, ML_HEADS + hd:ML_HEADS + hd + 1]
            c0 = c_sc[hd]
            n0 = n_sc[hd:hd + 1, :]
            m0 = m_sc[0:1, hd:hd + 1]
            d = jnp.where(causal, b_col - b_row + ig_row, -jnp.inf)
            inter = b_col + m0
            m = jnp.maximum(inter, jnp.max(d, axis=1, keepdims=True))
            w = jnp.exp(d - m)
            a = jnp.exp(inter - m)
            wqk = w * _dot_nt(q, k)
            num = a * _dot_nt(q, c0.astype(BF16)) + _dot(wqk.astype(BF16), v)
            qf = q.astype(F32)
            den = a * jnp.sum(qf * n0, axis=1, keepdims=True) + jnp.sum(wqk, axis=1, keepdims=True)
            h_ref[0, pl.ds(start, lc), sl] = num / jnp.maximum(jnp.abs(den), jnp.exp(-m))
            b_last = b_col[lc - 1:lc, :]
            dec_row = b_last - b_row + ig_row
            m_last = jnp.maximum(b_last + m0, jnp.max(dec_row, axis=1, keepdims=True))
            ws_col = jnp.exp(b_last - b_col + ig_col - m_last)
            a_last = jnp.exp(b_last + m0 - m_last)
            c_sc[hd] = a_last * c0 + _dot_tn((v.astype(F32) * ws_col).astype(BF16), k)
            n_sc[hd:hd + 1, :] = a_last * n0 + jnp.sum(ws_col * k.astype(F32), axis=0, keepdims=True)
            m_sc[0:1, hd:hd + 1] = m_last
        return carry

    lax.fori_loop(0, n_chunks, body, 0)
    c_ref[0] = c_sc[...]
    n_ref[0] = n_sc[...]
    m_ref[0] = m_sc[...]


def _mlstm(q, k, v, rf, cf, c0, n0, m0):
    b, t, w = q.shape
    nc = rf.shape[1]
    seq = pl.BlockSpec((1, t, w), lambda i: (i, 0, 0))
    st = [pl.BlockSpec((1,) + c0.shape[1:], lambda i: (i, 0, 0, 0)),
          pl.BlockSpec((1,) + n0.shape[1:], lambda i: (i, 0, 0)),
          pl.BlockSpec((1,) + m0.shape[1:], lambda i: (i, 0, 0))]
    return pl.pallas_call(
        _mlstm_kernel,
        grid=(b,),
        in_specs=[seq, seq, seq,
                  pl.BlockSpec((1,) + rf.shape[1:], lambda i: (i, 0, 0, 0)),
                  pl.BlockSpec((1,) + cf.shape[1:], lambda i: (i, 0, 0, 0))] + st,
        out_specs=[seq] + st,
        out_shape=[jax.ShapeDtypeStruct((b, t, w), F32), jax.ShapeDtypeStruct(c0.shape, F32),
                   jax.ShapeDtypeStruct(n0.shape, F32), jax.ShapeDtypeStruct(m0.shape, F32)],
        scratch_shapes=[pltpu.VMEM(c0.shape[1:], F32), pltpu.VMEM(n0.shape[1:], F32), pltpu.VMEM(m0.shape[1:], F32)],
        compiler_params=_cparams(("parallel",)),
        name="mlstm",
    )(q, k, v, rf, cf, c0, n0, m0)


def _mix_kernel(olat_ref, hml_ref, og_ref, res_ref, wuv_ref, mhg_ref, wo_ref, g_ref, b_ref, o_ref, o8_ref, *, alpha):
    omla = _dot(olat_ref[...], wuv_ref[...])
    hml = hml_ref[...]
    parts = []
    for hd in range(ML_HEADS):
        x = hml[:, hd * ML_DH:(hd + 1) * ML_DH]
        mu = jnp.mean(x, axis=-1, keepdims=True)
        xc = x - mu
        var = jnp.mean(xc * xc, axis=-1, keepdims=True)
        parts.append(xc * lax.rsqrt(var + EPS))
    oml = (jnp.concatenate(parts, axis=1) * mhg_ref[...]) * og_ref[...]
    cat = jnp.concatenate([omla.astype(BF16), oml.astype(BF16)], axis=1)
    a = _dot(cat, wo_ref[...])
    h1 = _layer_norm(alpha * res_ref[...] + a, g_ref[...], b_ref[...])
    o_ref[...] = h1
    tm = h1.shape[0]
    for c in range(h1.shape[1] // LANES):
        o8_ref[pl.ds(c, tm, stride=SUBLANES), :] = h1[:, c * LANES:(c + 1) * LANES]


def _mix(olat, hml, og, res, wuv, mhg, wo, g, b, alpha):
    n, d = res.shape
    tm = ROW_BLOCK
    rows = lambda w: pl.BlockSpec((tm, w), lambda i: (i, 0))
    return pl.pallas_call(
        functools.partial(_mix_kernel, alpha=alpha),
        grid=(n // tm,),
        in_specs=[rows(olat.shape[1]), rows(ML_W), rows(ML_W), rows(d), _full(wuv.shape), _full(mhg.shape),
                  _full(wo.shape), _full(g.shape), _full(b.shape)],
        out_specs=[rows(d), pl.BlockSpec((tm * SUBLANES, LANES), lambda i: (i, 0))],
        out_shape=[jax.ShapeDtypeStruct((n, d), F32), jax.ShapeDtypeStruct((n * SUBLANES, LANES), F32)],
        compiler_params=_cparams(("parallel",)),
        name="mix",
    )(olat, hml, og, res, wuv, mhg, wo, g, b)


def _res_ln_kernel(res_ref, y8_ref, g_ref, b_ref, o_ref, *, alpha):
    tm, d = res_ref.shape
    y = jnp.concatenate([y8_ref[pl.ds(c, tm, stride=SUBLANES), :] for c in range(d // LANES)], axis=1)
    o_ref[...] = _layer_norm(alpha * res_ref[...] + y, g_ref[...], b_ref[...])


def _res_ln(res, y, g, b, alpha, r0, nr):
    d = res.shape[1]
    tm = ROW_BLOCK
    b0 = r0 // tm
    return pl.pallas_call(
        functools.partial(_res_ln_kernel, alpha=alpha),
        grid=(nr // tm,),
        in_specs=[pl.BlockSpec((tm, d), lambda i: (b0 + i, 0)),
                  pl.BlockSpec((tm * SUBLANES, LANES), lambda i: (b0 + i, 0)), _full(g.shape), _full(b.shape)],
        out_specs=pl.BlockSpec((tm, d), lambda i: (i, 0)),
        out_shape=jax.ShapeDtypeStruct((nr, d), F32),
        compiler_params=_cparams(("parallel",)),
        name="res_ln",
    )(res, y, g, b)


def _topk_rounds(jobs):
    def body(kk, carry):
        for s_sc, n_rows, tie, vals_sc, idx_sc in jobs:
            s = s_sc[0:n_rows, :]
            m = jnp.max(s, axis=0, keepdims=True)
            i = jnp.min(jnp.where(s == m, tie, jnp.inf), axis=0, keepdims=True)
            vals_sc[pl.ds(kk, 1), :] = m
            idx_sc[pl.ds(kk, 1), :] = i
            s_sc[0:n_rows, :] = jnp.where(tie == i, -jnp.inf, s)
        return carry

    lax.fori_loop(0, PEER_TOPK, body, 0)


def _candidate_blocks():
    k, sub = PEER_TOPK, SUBLANES
    need = [k // (i + 1) for i in range(k)]
    blocks, covered = [], set()
    for i in range(k):
        if need[i] >= sub:
            nj = -(-need[i] // sub) * sub
            blocks.append(("row", i, nj))
            covered |= {(i, j) for j in range(nj)}
    for i0 in range(0, k, sub):
        rest = [i for i in range(i0, i0 + sub) if need[i] < sub]
        if rest:
            for j in range(max(need[i] for i in rest)):
                blocks.append(("col", j, i0, rest[0]))
                covered |= {(i, j) for i in range(rest[0], i0 + sub)}
    assert all((i, j) in covered for i in range(k) for j in range(k) if (i + 1) * (j + 1) <= k)
    return blocks


CAND_BLOCKS = _candidate_blocks()
CAND_ROWS = sum(b[2] if b[0] == "row" else SUBLANES for b in CAND_BLOCKS)


def _route_kernel(x_ref, whi_ref, wlo_ref, khi_ref, klo_ref, idx_ref, g_ref,
                  q_sc, s0_sc, s1_sc, c_sc, sv0, si0, sv1, si1, cv, ci):
    hd = pl.program_id(1)
    kq = PEER_DQ // 2
    tm = x_ref.shape[0]

    @pl.when(hd == 0)
    def _():
        x = x_ref[...]
        xh = x.astype(BF16)
        xl = (x - xh.astype(F32)).astype(BF16)
        q = _dot(xh, whi_ref[...]) + _dot(xh, wlo_ref[...]) + _dot(xl, whi_ref[...])
        for j in range(PEER_HEADS):
            q_sc[j] = q[:, j * PEER_DQ:(j + 1) * PEER_DQ]

    qh_all = q_sc[hd]
    for p, s_sc in enumerate((s0_sc, s1_sc)):
        qq = qh_all[:, p * kq:(p + 1) * kq]
        qh = qq.astype(BF16)
        ql = (qq - qh.astype(F32)).astype(BF16)
        kh, kl = khi_ref[0, p], klo_ref[0, p]
        s_sc[...] = _dot_nt(kh, qh) + _dot_nt(kh, ql) + _dot_nt(kl, qh)
    key_iota = lax.broadcasted_iota(jnp.int32, (N_KEYS, tm), 0).astype(F32)
    _topk_rounds([(s0_sc, N_KEYS, key_iota, sv0, si0), (s1_sc, N_KEYS, key_iota, sv1, si1)])

    row_iota = lambda n: lax.broadcasted_iota(jnp.int32, (n, tm), 0).astype(F32)
    flat_parts, off = [], 0
    for blk in CAND_BLOCKS:
        if blk[0] == "row":
            _, i, nj = blk
            c_sc[off:off + nj, :] = sv0[i:i + 1, :] + sv1[0:nj, :]
            flat_parts.append(row_iota(nj) + float(i * PEER_TOPK))
            off += nj
        else:
            _, j, i0, imin = blk
            ii = row_iota(SUBLANES) + float(i0)
            c_sc[off:off + SUBLANES, :] = jnp.where(ii >= float(imin), sv0[i0:i0 + SUBLANES, :] + sv1[j:j + 1, :], -jnp.inf)
            flat_parts.append(ii * float(PEER_TOPK) + float(j))
            off += SUBLANES
    _topk_rounds([(c_sc, CAND_ROWS, jnp.concatenate(flat_parts, axis=0), cv, ci)])
    cidx = ci[...]
    i1 = jnp.floor(cidx * (1.0 / PEER_TOPK))
    i2 = cidx - i1 * PEER_TOPK
    e1 = jnp.zeros_like(cidx)
    e2 = jnp.zeros_like(cidx)
    for i in range(PEER_TOPK):
        e1 = jnp.where(i1 == float(i), si0[i:i + 1, :], e1)
        e2 = jnp.where(i2 == float(i), si1[i:i + 1, :], e2)
    idx_ref[0] = (e1 * float(N_KEYS) + e2).astype(jnp.int32)
    cvv = cv[...]
    e = jnp.exp(cvv - cvv[0:1, :])
    g_ref[0] = e / jnp.sum(e, axis=0, keepdims=True)


def _route(x, whi, wlo, khi, klo):
    n, d = x.shape
    tm = ROW_BLOCK
    out_spec = pl.BlockSpec((1, PEER_TOPK, tm), lambda i, h: (h, 0, i))
    key_spec = pl.BlockSpec((1, 2, N_KEYS, PEER_DQ // 2), lambda i, h: (h, 0, 0, 0))
    tk = pltpu.VMEM((PEER_TOPK, tm), F32)
    return pl.pallas_call(
        _route_kernel,
        grid=(n // tm, PEER_HEADS),
        in_specs=[pl.BlockSpec((tm, d), lambda i, h: (i, 0)), _full(whi.shape), _full(wlo.shape), key_spec, key_spec],
        out_specs=[out_spec, out_spec],
        out_shape=[jax.ShapeDtypeStruct((PEER_HEADS, PEER_TOPK, n), jnp.int32),
                   jax.ShapeDtypeStruct((PEER_HEADS, PEER_TOPK, n), F32)],
        scratch_shapes=[pltpu.VMEM((PEER_HEADS, tm, PEER_DQ), F32), pltpu.VMEM((N_KEYS, tm), F32),
                        pltpu.VMEM((N_KEYS, tm), F32), pltpu.VMEM((CAND_ROWS, tm), F32), tk, tk, tk, tk, tk, tk],
        compiler_params=_cparams(("parallel", "arbitrary")),
        name="peer_route",
    )(x, whi, wlo, khi, klo)


ROW_WORDS = 4
SLOTS = PEER_HEADS * PEER_TOPK


def _gather_rows(idx_ref, tab_ref, g_sc, t):
    grp = 8
    for g in range(SLOTS // grp):
        idx_g = idx_ref.at[pl.ds(t * SLOTS + g * grp, grp)]
        for j in range(grp):
            r = g * grp + j
            e4 = pl.multiple_of(idx_g[j], ROW_WORDS)
            g_sc[pl.ds(ROW_WORDS * r, ROW_WORDS), :] = tab_ref[pl.ds(e4, ROW_WORDS), :]


def _token_pipeline(tb, depth, gather, contract, finish):
    for b in range(depth):
        gather(b, b)

    def body(i, carry):
        t = depth * i
        res = [contract(t + b, b) for b in range(depth)]
        for b in range(depth):
            gather(jnp.minimum(t + depth + b, tb - 1), b)
        for b in range(depth):
            finish(t + b, res[b])
        return carry

    lax.fori_loop(0, tb // depth, body, 0)


def _diag_mask():
    rows16 = 2 * SUBLANES
    sub = lax.broadcasted_iota(jnp.int32, (rows16, SUBLANES * SLOTS), 0)
    lane = lax.broadcasted_iota(jnp.int32, (rows16, SUBLANES * SLOTS), 1)
    return _imod(lane, SUBLANES) == _imod(sub, SUBLANES), sub < SUBLANES


def _hi_lo_rows(x8):
    hi = x8.astype(BF16).astype(F32)
    return jnp.concatenate([hi, x8 - hi], axis=0).astype(BF16)


def _peer_u_kernel(idx_ref, x_ref, tab_ref, sel_ref, act_ref, *scratch):
    bufs, d_sc = scratch[:-1], scratch[-1]
    tb = x_ref.shape[0]
    diag, _ = _diag_mask()

    def gather(t, b):
        _gather_rows(idx_ref, tab_ref, bufs[b], t)

    def contract(t, b):
        gb = pltpu.bitcast(bufs[b][...], BF16)
        return _dot_nt(_hi_lo_rows(x_ref[t]), gb)

    def finish(t, d):
        d_sc[pl.ds(t, 1), :] = jnp.sum(jnp.where(diag, d, 0.0), axis=0, keepdims=True)

    _token_pipeline(tb, len(bufs), gather, contract, finish)
    act_ref[...] = _dot_exact_rhs01(d_sc[...], sel_ref[...])


def _peer_v_kernel(idx_ref, act_ref, gate_ref, tab_ref, selt_ref, o_ref, *scratch):
    bufs, w_sc = scratch[:-1], scratch[-1]
    tb = act_ref.shape[0]
    w = gate_ref[...] * jax.nn.gelu(act_ref[...], approximate=True)
    w_sc[...] = _dot_exact_rhs01(w, selt_ref[...])
    diag, top = _diag_mask()

    def gather(t, b):
        _gather_rows(idx_ref, tab_ref, bufs[b], t)

    def contract(t, b):
        gb = pltpu.bitcast(bufs[b][...], BF16)
        w16 = jnp.where(diag, jnp.broadcast_to(w_sc[pl.ds(t, 1), :], diag.shape), 0.0)
        hi = w16.astype(BF16).astype(F32)
        lhs = jnp.where(top, hi, w16 - hi).astype(BF16)
        return _dot(lhs, gb)

    def finish(t, o16):
        o_ref[pl.ds(pl.multiple_of(t * SUBLANES, SUBLANES), SUBLANES), :] = o16[0:SUBLANES] + o16[SUBLANES:]

    _token_pipeline(tb, len(bufs), gather, contract, finish)


def _table_spec(shape):
    return pl.BlockSpec(shape, lambda i: (0, 0), pipeline_mode=pl.Buffered(1))


def _peer_scratch(tb):
    gbuf = pltpu.VMEM((ROW_WORDS * SLOTS, LANES), jnp.uint32)
    return [gbuf] * PEER_PIPE + [pltpu.VMEM((tb, SUBLANES * SLOTS), F32)]


def _peer_u(idx4, x8, tab, sel):
    n = x8.shape[0]
    tb = PEER_TB
    return pl.pallas_call(
        _peer_u_kernel,
        grid=(n // tb,),
        in_specs=[pl.BlockSpec((tb * SLOTS,), lambda i: (i,), memory_space=pltpu.SMEM),
                  pl.BlockSpec((tb, SUBLANES, LANES), lambda i: (i, 0, 0)),
                  _table_spec(tab.shape), _full(sel.shape)],
        out_specs=pl.BlockSpec((tb, SLOTS), lambda i: (i, 0)),
        out_shape=jax.ShapeDtypeStruct((n, SLOTS), F32),
        scratch_shapes=_peer_scratch(tb),
        compiler_params=_cparams(("parallel",), vmem=56 * 1024 * 1024),
        name="peer_u",
    )(idx4, x8, tab, sel)


def _peer_v(idx4, act, gate, tab, selt):
    n = act.shape[0]
    tb = PEER_TB
    return pl.pallas_call(
        _peer_v_kernel,
        grid=(n // tb,),
        in_specs=[pl.BlockSpec((tb * SLOTS,), lambda i: (i,), memory_space=pltpu.SMEM),
                  pl.BlockSpec((tb, SLOTS), lambda i: (i, 0)), pl.BlockSpec((tb, SLOTS), lambda i: (i, 0)),
                  _table_spec(tab.shape), _full(selt.shape)],
        out_specs=pl.BlockSpec((tb * SUBLANES, LANES), lambda i: (i, 0)),
        out_shape=jax.ShapeDtypeStruct((n * SUBLANES, LANES), F32),
        scratch_shapes=_peer_scratch(tb),
        compiler_params=_cparams(("parallel",), vmem=56 * 1024 * 1024),
        name="peer_v",
    )(idx4, act, gate, tab, selt)


def _pack_kernel(t_ref, o_ref, tile_sc):
    r, d = t_ref.shape
    x = t_ref[...]
    for c in range(d // LANES):
        tile_sc[pl.ds(c, r, stride=SUBLANES), :] = x[:, c * LANES:(c + 1) * LANES]
    o_ref[...] = pltpu.bitcast(tile_sc[...].astype(BF16), jnp.uint32)


def _pack_table(t):
    e, d = t.shape
    rb = ROW_BLOCK
    return pl.pallas_call(
        _pack_kernel,
        grid=(e // rb,),
        in_specs=[pl.BlockSpec((rb, d), lambda i: (i, 0))],
        out_specs=pl.BlockSpec((rb * ROW_WORDS, LANES), lambda i: (i, 0)),
        out_shape=jax.ShapeDtypeStruct((e * ROW_WORDS, LANES), jnp.uint32),
        scratch_shapes=[pltpu.VMEM((rb * d // LANES, LANES), F32)],
        compiler_params=_cparams(("parallel",)),
        name="pack_table",
    )(t)


def _peer(x, x8, whi, wlo, khi, klo, tab_u, tab_v):
    n, d = x.shape
    idx_t, gate_t = _route(x, whi, wlo, khi, klo)
    idx4 = (idx_t.reshape(SLOTS, n).T * ROW_WORDS).reshape(n * SLOTS)
    gate = gate_t.reshape(SLOTS, n).T
    grp = jnp.arange(SUBLANES * SLOTS) // SUBLANES
    sel = (grp[:, None] == jnp.arange(SLOTS)[None, :]).astype(BF16)
    act = _peer_u(idx4, x8.reshape(n, SUBLANES, LANES), tab_u, sel)
    return _peer_v(idx4, act, gate, tab_v, sel.T)


def _pad_rows(x, n):
    return jnp.pad(x, ((0, n - x.shape[0]),) + ((0, 0),) * (x.ndim - 1))


def _rope_tables(pos):
    half = QK_ROPE // 2
    inv = ROPE_THETA ** (-jnp.arange(half, dtype=F32) / half)
    ang = pos.astype(F32)[:, None] * inv
    reps = LANES // half
    return jnp.tile(jnp.cos(ang), (1, reps)), jnp.tile(jnp.sin(ang), (1, reps))


def _layer_weights(w_in, w_qb, w_kvb):
    d = w_in.shape[0]
    o_ckv = Q_RANK
    o_kr = o_ckv + KV_RANK
    o_mq = o_kr + QK_ROPE
    o_g = o_mq + 4 * ML_W
    small = jnp.concatenate([w_in[:, o_kr:o_mq], w_in[:, o_g:o_g + 2 * ML_HEADS],
                             jnp.zeros((d, LANES - QK_ROPE - 2 * ML_HEADS), w_in.dtype)], axis=1)
    win = jnp.concatenate([w_in[:, :o_kr], small, w_in[:, o_mq:o_g]], axis=1).astype(BF16)
    half = QK_ROPE // 2
    wq = w_qb.reshape(Q_RANK, MLA_HEADS, QK_NOPE + QK_ROPE)
    wqb = jnp.concatenate([wq[..., :QK_NOPE].reshape(Q_RANK, -1), wq[..., QK_NOPE:QK_NOPE + half].reshape(Q_RANK, -1),
                           wq[..., QK_NOPE + half:].reshape(Q_RANK, -1)], axis=1).astype(BF16)
    wkv = w_kvb.reshape(KV_RANK, MLA_HEADS, QK_NOPE + V_DIM)
    eye = jnp.eye(MLA_HEADS, dtype=w_kvb.dtype)
    wuk = jnp.einsum('chd,hg->hdgc', wkv[..., :QK_NOPE], eye).reshape(MLA_HEADS * QK_NOPE, MLA_HEADS * KV_RANK)
    wuv = jnp.einsum('chv,hg->hcgv', wkv[..., QK_NOPE:], eye).reshape(MLA_HEADS * KV_RANK, MLA_W)
    return win, wqb, wuk.astype(BF16), wuv.astype(BF16)


def _hi_lo(w):
    hi = w.astype(BF16)
    return hi, (w - hi.astype(F32)).astype(BF16)


def kernel(x_prompt, x_sample, cache_ckv, cache_kpe, page_table, state_C, state_n, state_m, meta_tokens,
           ln_in_g, ln_in_b, w_in, b_gates, q_norm_g, w_qb, kv_norm_g, w_kvb, mh_norm_g, w_o, ln1_g, ln1_b,
           w_pq, sub_keys, expert_u, expert_v, ln2_g, ln2_b):
    B, seq, D = x_prompt.shape
    DB, S = x_sample.shape[:2]
    depth = w_in.shape[0]
    dt = x_prompt.dtype
    T = seq + N_META
    page = cache_ckv.shape[2]
    past_len = page_table.shape[1] * page
    alpha = (2.0 * depth) ** 0.25
    row = lambda a: a.reshape(1, -1)

    n_pt, n_s = B * seq, DB * S
    assert n_pt % ROW_BLOCK == 0, "prompt rows must fill whole row blocks"
    m0 = n_pt + n_s
    n_small = -(-(n_s + N_META) // ROW_BLOCK) * ROW_BLOCK
    n_pad = n_pt + n_small
    x_small = _pad_rows(jnp.concatenate([x_sample.reshape(n_s, D), meta_tokens.astype(dt)], axis=0), n_small)
    pos = jnp.concatenate([jnp.tile(N_META + jnp.arange(seq), B), jnp.tile(past_len + jnp.arange(S), DB),
                           jnp.arange(N_META), jnp.zeros((n_pad - m0 - N_META,), jnp.int32)])
    cos_t, sin_t = _rope_tables(pos)

    t_att = -(-T // ATT_TK) * ATT_TK
    nc_p = -(-T // ML_CHUNK)
    t_ml = nc_p * ML_CHUNK
    new_pad = 16

    def prompt_seqs(a, t_total):
        w = a.shape[-1]
        parts = [jnp.broadcast_to(a[m0:m0 + N_META][None], (B, N_META, w)), a[:n_pt].reshape(B, seq, w)]
        if t_total > T:
            parts.append(jnp.zeros((B, t_total - T, w), a.dtype))
        return jnp.concatenate(parts, axis=1)

    def sample_seqs(a, t_total):
        return jnp.pad(a[n_pt:m0].reshape(DB, S, a.shape[-1]), ((0, 0), (0, t_total - S), (0, 0)))

    def flat_rows(prompt_part, sample_part, meta_part):
        w = meta_part.shape[-1]
        return _pad_rows(jnp.concatenate([prompt_part.reshape(n_pt, w), sample_part.reshape(n_s, w), meta_part], axis=0),
                         n_pad)

    def gate_forms(g_seq, n_valid):
        nb, t_total = g_seq.shape[:2]
        nc = t_total // ML_CHUNK
        gp = g_seq.reshape(nb * nc, ML_CHUNK, 2 * ML_HEADS).transpose(0, 2, 1)
        valid = jnp.broadcast_to((jnp.arange(t_total) < n_valid).astype(F32)[None], (nb, t_total)).reshape(nb * nc, 1, ML_CHUNK)
        rf = _gate_prep(gp, bias, valid).reshape(nb, nc, 2 * ML_HEADS, ML_CHUNK)
        return rf, rf.transpose(0, 1, 3, 2)

    outs = {k: [] for k in ("ckv_p", "kpe_p", "C_p", "n_p", "m_p", "ckv_s", "kpe_s", "C_s", "n_s", "m_s")}
    h_main, h_small = x_prompt.reshape(n_pt, D), x_small
    for l in range(depth):
        win, wqb, wuk, wuv = _layer_weights(w_in[l], w_qb[l], w_kvb[l])
        (h, ckv, kpe, zs, qlat, qrope, mq, mk, mv, og) = _proj(
            h_main, h_small, cos_t, sin_t, row(ln_in_g), row(ln_in_b), win, row(q_norm_g[l]), wqb, wuk,
            row(kv_norm_g[l]), apply_ln=(l == 0))
        gates = zs[:, QK_ROPE:QK_ROPE + 2 * ML_HEADS]
        bias = b_gates[l].astype(F32).reshape(1, 2 * ML_HEADS, 1)

        half = QK_ROPE // 2
        dk = KV_RANK + QK_ROPE
        qcat = jnp.concatenate([qlat.reshape(n_pad, MLA_HEADS, KV_RANK),
                                qrope[:, :LANES].reshape(n_pad, MLA_HEADS, half),
                                qrope[:, LANES:].reshape(n_pad, MLA_HEADS, half)], axis=-1)
        kcat = jnp.concatenate([ckv, kpe], axis=-1).astype(BF16)

        q_p = prompt_seqs(qcat.reshape(n_pad, MLA_HEADS * dk), t_att).reshape(B, t_att, MLA_HEADS, dk)
        k_p = prompt_seqs(kcat, t_att)
        kt_p = k_p.reshape(B, t_att // ATT_TK, ATT_TK, dk).transpose(0, 1, 3, 2)
        v_p = jnp.concatenate([k_p[..., :KV_RANK], jnp.ones((B, t_att, KV_RANK), BF16)], axis=-1)
        o_p = _attn_prompt(q_p.transpose(0, 2, 1, 3), kt_p, v_p)

        q_s = qcat[n_pt:m0].reshape(DB, S, MLA_HEADS, dk).transpose(0, 2, 1, 3).reshape(DB, MLA_HEADS * S, dk)
        o_s = _attn_sample(page_table, q_s, sample_seqs(kcat, new_pad), cache_ckv[l],
                           jnp.swapaxes(cache_kpe[l], -1, -2), S)
        olat = flat_rows(o_p[:, :, N_META:T].transpose(0, 2, 1, 3),
                         o_s.reshape(DB, MLA_HEADS, S, KV_RANK).transpose(0, 2, 1, 3),
                         o_p[0, :, :N_META].transpose(1, 0, 2).reshape(N_META, MLA_HEADS * KV_RANK))

        rf_p, cf_p = gate_forms(prompt_seqs(gates, t_ml), T)
        h_p, C_p, nn_p, m_p = _mlstm(prompt_seqs(mq, t_ml), prompt_seqs(mk, t_ml), prompt_seqs(mv, t_ml), rf_p, cf_p,
                                     jnp.zeros((B, ML_HEADS, ML_DH, ML_DH), F32), jnp.zeros((B, ML_HEADS, ML_DH), F32),
                                     jnp.zeros((B, 1, ML_HEADS), F32))
        rf_s, cf_s = gate_forms(sample_seqs(gates, ML_CHUNK), S)
        h_s, C_s, nn_s, m_s = _mlstm(sample_seqs(mq, ML_CHUNK), sample_seqs(mk, ML_CHUNK), sample_seqs(mv, ML_CHUNK),
                                     rf_s, cf_s, state_C[l].astype(F32), state_n[l].astype(F32),
                                     state_m[l].astype(F32).reshape(DB, 1, ML_HEADS))
        hml = flat_rows(h_p[:, N_META:T], h_s[:, :S], h_p[0, :N_META])

        h1, h1_tiles = _mix(olat, hml, og, h, wuv, row(mh_norm_g[l]), w_o[l].astype(BF16), row(ln1_g[l]), row(ln1_b[l]), alpha)
        whi, wlo = _hi_lo(w_pq[l])
        khi, klo = _hi_lo(sub_keys[l])
        y = _peer(h1, h1_tiles, whi, wlo, khi, klo, _pack_table(expert_u[l]), _pack_table(expert_v[l]))
        h_main = _res_ln(h1, y, row(ln2_g[l]), row(ln2_b[l]), alpha, 0, n_pt)
        h_small = _res_ln(h1, y, row(ln2_g[l]), row(ln2_b[l]), alpha, n_pt, n_small)

        outs["ckv_p"].append(prompt_seqs(ckv, T))
        outs["kpe_p"].append(prompt_seqs(kpe, T))
        outs["C_p"].append(C_p.astype(dt)); outs["n_p"].append(nn_p.astype(dt)); outs["m_p"].append(m_p.reshape(B, ML_HEADS).astype(dt))
        outs["ckv_s"].append(ckv[n_pt:m0].reshape(DB, S, KV_RANK))
        outs["kpe_s"].append(kpe[n_pt:m0].reshape(DB, S, QK_ROPE))
        outs["C_s"].append(C_s.astype(dt)); outs["n_s"].append(nn_s.astype(dt)); outs["m_s"].append(m_s.reshape(DB, ML_HEADS).astype(dt))

    y_prompt = h_main.reshape(B, seq, D)
    y_sample = h_small[:n_s].reshape(DB, S, D)
    st = lambda k: jnp.stack(outs[k])
    return (y_prompt, y_sample, st("ckv_p"), st("kpe_p"), st("C_p"), st("n_p"), st("m_p"),
            st("ckv_s"), st("kpe_s"), st("C_s"), st("n_s"), st("m_s"))
```
